```python
import math
import jax, jax.numpy as jnp
from jax import lax
import numpy as np

D_MODEL = 2048
BATCH = 4
SEQ = 8192
DEPTH = 1
DEC_BATCH = 16
DEC_SEQ = 32
PAST_LEN = 2048

CHUNK = 64
Q_BLOCK = 128
D_MIX = D_MODEL
DA_WIDTH = D_MIX // 2
DA_HEADS = 8
DA_V_DIM = DA_WIDTH // DA_HEADS
DA_QK_DIM = DA_V_DIM // 2
RW_WIDTH = D_MIX - DA_WIDTH
RW_HEAD = 64
RW_HEADS = RW_WIDTH // RW_HEAD
RW_LORA_W = 64
RW_LORA_A = 64
ROPE_THETA = 10000.0
NORM_EPS = 1e-6
LNX_EPS = 64e-5
RW_SHIFT_COLS = 3 * RW_WIDTH + RW_LORA_W + RW_LORA_A
IN_COLS = 4 * DA_WIDTH + RW_SHIFT_COLS + RW_WIDTH
NEG_INF = -1e30

kernel_name = "hymba_diffattn_rwkv7_stream_step"


def rms_norm(x, g):
    xf = x.astype(jnp.float32)
    xf = xf * lax.rsqrt(jnp.mean(xf * xf, axis=-1, keepdims=True) + NORM_EPS)
    return (xf * g.astype(jnp.float32)).astype(x.dtype)


def rope(x, pos):
    half = x.shape[-1] // 2
    inv_freq = ROPE_THETA ** (-jnp.arange(half, dtype=jnp.float32) / half)
    ang = pos.astype(jnp.float32)[:, None] * inv_freq[None, :]
    cos = jnp.cos(ang)[None, :, None, None, :]
    sin = jnp.sin(ang)[None, :, None, None, :]
    xf = x.astype(jnp.float32)
    x1, x2 = xf[..., :half], xf[..., half:]
    return jnp.concatenate([x1 * cos - x2 * sin, x2 * cos + x1 * sin], axis=-1).astype(x.dtype)


def diff_attention(q, k, v, q_pos, k_pos, lam):
    s = jnp.einsum('bqhcd,bkhcd->bhcqk', q, k, preferred_element_type=jnp.float32) * (DA_QK_DIM ** -0.5)
    mask = (k_pos[None, :] // CHUNK) <= (q_pos[:, None] // CHUNK)
    s = jnp.where(mask, s, NEG_INF)
    p = jax.nn.softmax(s, axis=-1)
    attn = p[:, :, 0] - lam * p[:, :, 1]
    return jnp.einsum('bhqk,bkhd->bqhd', attn.astype(v.dtype), v)


def wkv_scan(r, decay, k, v, a_vec, b_vec, s0):
    def step(S, inp):
        r_t, w_t, k_t, v_t, a_t, b_t = inp
        sa = jnp.einsum('bhij,bhj->bhi', S, a_t)
        S = S * w_t[:, :, None, :] + sa[..., None] * b_t[:, :, None, :] + v_t[..., None] * k_t[:, :, None, :]
        y = jnp.einsum('bhij,bhj->bhi', S, r_t)
        return S, y
    xs = tuple(jnp.moveaxis(t, 1, 0) for t in (r, decay, k, v, a_vec, b_vec))
    s_final, ys = lax.scan(step, s0, xs)
    return jnp.moveaxis(ys, 0, 1), s_final


def trunk_layer(x, past_k, past_v, wkv0, shift0, layer_idx, norm_g, w_in, q_norm_g, k_norm_g,
                lambda_q1, lambda_k1, lambda_q2, lambda_k2, subln_g, shift_mu, w0, w_up, a0, a_up,
                k_k, k_a, r_k, lnx_g, lnx_b, w_out):
    B, T, _ = x.shape
    P = past_k.shape[1]
    pos = P + jnp.arange(T)
    xn = rms_norm(x, norm_g)
    proj = xn @ w_in
    q, k, v, g_att, p_rw, g_rw = jnp.split(
        proj, [DA_WIDTH, 2 * DA_WIDTH, 3 * DA_WIDTH, 4 * DA_WIDTH, 4 * DA_WIDTH + RW_SHIFT_COLS], axis=-1)

    q = rope(rms_norm(q.reshape(B, T, DA_HEADS, 2, DA_QK_DIM), q_norm_g), pos)
    k = rope(rms_norm(k.reshape(B, T, DA_HEADS, 2, DA_QK_DIM), k_norm_g), pos)
    v = v.reshape(B, T, DA_HEADS, DA_V_DIM)
    keys = jnp.concatenate([past_k, k], axis=1)
    vals = jnp.concatenate([past_v, v], axis=1)
    k_pos = jnp.arange(P + T)
    lam_init = 0.8 - 0.6 * math.exp(-0.3 * layer_idx)
    f32 = jnp.float32
    lam = (jnp.exp(jnp.sum(lambda_q1.astype(f32) * lambda_k1.astype(f32)))
           - jnp.exp(jnp.sum(lambda_q2.astype(f32) * lambda_k2.astype(f32))) + lam_init)
    if T > Q_BLOCK and T % Q_BLOCK == 0:
        nb = T // Q_BLOCK
        qb = q.reshape(B, nb, Q_BLOCK, DA_HEADS, 2, DA_QK_DIM).swapaxes(0, 1)
        pb = pos.reshape(nb, Q_BLOCK)
        ob = lax.map(lambda qp: diff_attention(qp[0], keys, vals, qp[1], k_pos, lam), (qb, pb))
        o = ob.swapaxes(0, 1).reshape(B, T, DA_HEADS, DA_V_DIM)
    else:
        o = diff_attention(q, keys, vals, pos, k_pos, lam)
    o = rms_norm(o, subln_g) * (1.0 - lam_init)
    att_out = o.reshape(B, T, DA_WIDTH) * jax.nn.silu(g_att)

    prev = jnp.concatenate([shift0, p_rw[:, :-1]], axis=1)
    xs = p_rw + shift_mu * (prev - p_rw)
    r, kr, vr, wl, al = jnp.split(xs, [RW_WIDTH, 2 * RW_WIDTH, 3 * RW_WIDTH, 3 * RW_WIDTH + RW_LORA_W], axis=-1)
    w = -jax.nn.softplus(-(w0 + jnp.tanh(wl) @ w_up)) - 0.5
    decay = jnp.exp(-jnp.exp(w.astype(f32)))
    a = jax.nn.sigmoid((a0 + al @ a_up).astype(f32))
    kk = (kr * k_k).astype(f32).reshape(B, T, RW_HEADS, RW_HEAD)
    kk = kk / jnp.maximum(jnp.linalg.norm(kk, axis=-1, keepdims=True), 1e-12)
    kmod = kr.astype(f32) * (1.0 + (a - 1.0) * k_a.astype(f32))
    hs = (B, T, RW_HEADS, RW_HEAD)
    r_h = r.astype(f32).reshape(hs)
    k_h = kmod.reshape(hs)
    v_h = vr.astype(f32).reshape(hs)
    a_h = a.reshape(hs)
    y, s_final = wkv_scan(r_h, decay.reshape(hs), k_h, v_h, -kk, kk * a_h, wkv0.astype(f32))
    mu = jnp.mean(y, axis=-1, keepdims=True)
    var = jnp.mean(jnp.square(y - mu), axis=-1, keepdims=True)
    y = ((y - mu) * lax.rsqrt(var + LNX_EPS)).reshape(B, T, RW_WIDTH) * lnx_g.astype(f32) + lnx_b.astype(f32)
    bonus = jnp.sum(r_h * k_h * r_k.astype(f32), axis=-1, keepdims=True) * v_h
    y = y + bonus.reshape(B, T, RW_WIDTH)
    rw_out = y.astype(x.dtype) * jax.nn.silu(g_rw)

    out = jnp.concatenate([att_out, rw_out], axis=-1) @ w_out
    return x + out, k, v, s_final.astype(wkv0.dtype), p_rw[:, -1:]


def setup_inputs(seed: int = 0) -> dict:
    key = jax.random.key(seed)
    ks = jax.random.split(key, 32)
    n = lambda i, shape: jax.random.normal(ks[i], shape, jnp.float32)
    return {
        "x_prompt": n(0, (BATCH, SEQ, D_MODEL)),
        "x_sample": n(1, (DEC_BATCH, DEC_SEQ, D_MODEL)),
        "cache_attn_k": n(2, (DEPTH, DEC_BATCH, PAST_LEN, DA_HEADS, 2, DA_QK_DIM)),
        "cache_attn_v": n(3, (DEPTH, DEC_BATCH, PAST_LEN, DA_HEADS, DA_V_DIM)),
        "state_rwkv_wkv": 0.3 * n(4, (DEPTH, DEC_BATCH, RW_HEADS, RW_HEAD, RW_HEAD)),
        "state_rwkv_shift": n(5, (DEPTH, DEC_BATCH, 1, RW_SHIFT_COLS)),
        "norm_g": 1.0 + 0.02 * n(6, (DEPTH, D_MODEL)),
        "w_in": n(7, (DEPTH, D_MODEL, IN_COLS)) * D_MODEL ** -0.5,
        "q_norm_g": 1.0 + 0.02 * n(8, (DEPTH, DA_QK_DIM)),
        "k_norm_g": 1.0 + 0.02 * n(9, (DEPTH, DA_QK_DIM)),
        "lambda_q1": 0.1 * n(10, (DEPTH, DA_QK_DIM)),
        "lambda_k1": 0.1 * n(11, (DEPTH, DA_QK_DIM)),
        "lambda_q2": 0.1 * n(12, (DEPTH, DA_QK_DIM)),
        "lambda_k2": 0.1 * n(13, (DEPTH, DA_QK_DIM)),
        "subln_g": 1.0 + 0.02 * n(14, (DEPTH, DA_V_DIM)),
        "shift_mu": jax.random.uniform(ks[15], (DEPTH, RW_SHIFT_COLS), jnp.float32),
        "w0": 0.3 * n(16, (DEPTH, RW_WIDTH)),
        "w_up": 0.5 * n(17, (DEPTH, RW_LORA_W, RW_WIDTH)) * RW_LORA_W ** -0.5,
        "a0": 0.1 * n(18, (DEPTH, RW_WIDTH)),
        "a_up": 0.5 * n(19, (DEPTH, RW_LORA_A, RW_WIDTH)) * RW_LORA_A ** -0.5,
        "k_k": 0.85 + 0.02 * n(20, (DEPTH, RW_WIDTH)),
        "k_a": 1.0 + 0.02 * n(21, (DEPTH, RW_WIDTH)),
        "r_k": 0.1 * n(22, (DEPTH, RW_HEADS, RW_HEAD)),
        "lnx_g": 1.0 + 0.02 * n(23, (DEPTH, RW_WIDTH)),
        "lnx_b": 0.02 * n(24, (DEPTH, RW_WIDTH)),
        "w_out": n(25, (DEPTH, D_MIX, D_MODEL)) * D_MIX ** -0.5,
    }


def reference(x_prompt, x_sample, cache_attn_k, cache_attn_v, state_rwkv_wkv, state_rwkv_shift,
              norm_g, w_in, q_norm_g, k_norm_g, lambda_q1, lambda_k1, lambda_q2, lambda_k2, subln_g,
              shift_mu, w0, w_up, a0, a_up, k_k, k_a, r_k, lnx_g, lnx_b, w_out):
    yp, ys = x_prompt, x_sample
    B = x_prompt.shape[0]
    dt = x_prompt.dtype
    kp_l, vp_l, sp_l, hp_l, ks_l, vs_l, ss_l, hs_l = [], [], [], [], [], [], [], []
    for l in range(DEPTH):
        lw = (norm_g[l], w_in[l], q_norm_g[l], k_norm_g[l], lambda_q1[l], lambda_k1[l], lambda_q2[l],
              lambda_k2[l], subln_g[l], shift_mu[l], w0[l], w_up[l], a0[l], a_up[l], k_k[l], k_a[l],
              r_k[l], lnx_g[l], lnx_b[l], w_out[l])
        yp, kp, vp, sp, hp = trunk_layer(
            yp, jnp.zeros((B, 0, DA_HEADS, 2, DA_QK_DIM), dt), jnp.zeros((B, 0, DA_HEADS, DA_V_DIM), dt),
            jnp.zeros((B, RW_HEADS, RW_HEAD, RW_HEAD), dt), jnp.zeros((B, 1, RW_SHIFT_COLS), dt), l, *lw)
        ys, ksn, vsn, ssn, hsn = trunk_layer(
            ys, cache_attn_k[l], cache_attn_v[l], state_rwkv_wkv[l], state_rwkv_shift[l], l, *lw)
        kp_l.append(kp); vp_l.append(vp); sp_l.append(sp); hp_l.append(hp)
        ks_l.append(ksn); vs_l.append(vsn); ss_l.append(ssn); hs_l.append(hsn)
    return (yp, ys, jnp.stack(kp_l), jnp.stack(vp_l), jnp.stack(sp_l), jnp.stack(hp_l),
            jnp.stack(ks_l), jnp.stack(vs_l), jnp.stack(ss_l), jnp.stack(hs_l))
```

```python
import functools
import math

import jax
import jax.numpy as jnp
from jax import lax
from jax.experimental import pallas as pl
from jax.experimental.pallas import tpu as pltpu

F32 = jnp.float32
BF16 = jnp.bfloat16

D_MODEL = 2048
CHUNK = 64
DA_HEADS = 8
DA_V_DIM = 128
DA_QK_DIM = 64
DA_WIDTH = DA_HEADS * DA_V_DIM
RW_WIDTH = 1024
RW_HEAD = 64
RW_HEADS = RW_WIDTH // RW_HEAD
RW_LORA = 64
RW_SHIFT_COLS = 3 * RW_WIDTH + 2 * RW_LORA
IN_COLS = 4 * DA_WIDTH + RW_SHIFT_COLS + RW_WIDTH
ROPE_THETA = 10000.0
NORM_EPS = 1e-6
LNX_EPS = 64e-5
NEG_INF = -1e30

LANES = 128
N_COL_BLOCKS = IN_COLS // LANES
QB0, KB0, VB0, GAB0, RB0, KRB0, VRB0, GRB0, WAB0 = 0, 8, 16, 24, 32, 40, 48, 56, 64
RW_BLOCKS = RW_WIDTH // LANES
SHIFT_BLOCKS = RW_SHIFT_COLS // LANES

VMEM_LIMIT = 56 * 1024 * 1024

HI = lax.Precision.HIGHEST


def _cparams(sem):
    return pltpu.CompilerParams(dimension_semantics=sem, vmem_limit_bytes=VMEM_LIMIT)


def _in_proj_kernel(x_ref, g_ref, w_ref, o_ref, xn_ref):
    @pl.when(pl.program_id(1) == 0)
    def _():
        xf = x_ref[...]
        ms = jnp.mean(xf * xf, axis=-1, keepdims=True)
        xn_ref[...] = (xf * lax.rsqrt(ms + NORM_EPS) * g_ref[...]).astype(BF16)

    acc = jnp.dot(xn_ref[...], w_ref[...], preferred_element_type=F32)
    for c in range(o_ref.shape[0]):
        o_ref[c] = acc[:, c * LANES:(c + 1) * LANES]


def _in_proj(x2, norm_g, w_bf, *, tm, bn):
    n = x2.shape[0]
    return pl.pallas_call(
        _in_proj_kernel,
        out_shape=jax.ShapeDtypeStruct((N_COL_BLOCKS, n, LANES), F32),
        grid=(n // tm, N_COL_BLOCKS // bn),
        in_specs=[
            pl.BlockSpec((tm, D_MODEL), lambda i, j: (i, 0)),
            pl.BlockSpec((1, D_MODEL), lambda i, j: (0, 0)),
            pl.BlockSpec((D_MODEL, bn * LANES), lambda i, j: (0, j)),
        ],
        out_specs=pl.BlockSpec((bn, tm, LANES), lambda i, j: (j, i, 0)),
        scratch_shapes=[pltpu.VMEM((tm, D_MODEL), BF16)],
        compiler_params=_cparams(("parallel", "arbitrary")),
        name="in_proj",
    )(x2, norm_g, w_bf)


def _group_sums(x, lo):
    s_lo = jnp.sum(jnp.where(lo, x, 0.0), axis=-1, keepdims=True)
    s_hi = jnp.sum(jnp.where(lo, 0.0, x), axis=-1, keepdims=True)
    return jnp.where(lo, s_lo, s_hi)


def _qkv_kernel(p_ref, gq_ref, gk_ref, invf_ref, kout_ref, vout_ref, qbf_ref, kbf_ref, vbf_ref,
                *, tq, t_len, pos0):
    i = pl.program_id(0)
    row = lax.broadcasted_iota(jnp.int32, (tq, LANES), 0)
    lane = lax.broadcasted_iota(jnp.int32, (tq, LANES), 1)
    t = jnp.bitwise_and(i * tq + row, t_len - 1)
    ang = (pos0 + t).astype(F32) * invf_ref[...]
    cos = jnp.cos(ang)
    sin = jnp.sin(ang)
    first_half = jnp.bitwise_and(lane, DA_QK_DIM - 1) < DA_QK_DIM // 2
    sin_signed = jnp.where(first_half, -sin, sin)
    lo = lane < DA_QK_DIM

    def norm_rope(x, g):
        ms = _group_sums(x * x, lo) * (1.0 / DA_QK_DIM)
        xn = x * lax.rsqrt(ms + NORM_EPS) * g
        rot = jnp.where(first_half, pltpu.roll(xn, LANES - 32, 1), pltpu.roll(xn, 32, 1))
        return xn * cos + rot * sin_signed

    gq = gq_ref[...]
    gk = gk_ref[...]
    for h in range(DA_HEADS):
        q = norm_rope(p_ref[QB0 + h], gq)
        qbf_ref[h] = (q * (DA_QK_DIM ** -0.5)).astype(BF16)
        k = norm_rope(p_ref[KB0 + h], gk)
        kout_ref[:, h * LANES:(h + 1) * LANES] = k
        kbf_ref[h] = k.astype(BF16)
        v = p_ref[VB0 + h]
        vout_ref[:, h * LANES:(h + 1) * LANES] = v
        vbf_ref[h] = v.astype(BF16)


def _qkv_prep(proj, gq, gk, invf, *, t_len, pos0, tq):
    n = proj.shape[1]
    kern = functools.partial(_qkv_kernel, tq=tq, t_len=t_len, pos0=pos0)
    hm = jax.ShapeDtypeStruct((DA_HEADS, n, LANES), BF16)
    row_major = jax.ShapeDtypeStruct((n, DA_WIDTH), F32)
    hm_spec = pl.BlockSpec((DA_HEADS, tq, LANES), lambda i: (0, i, 0))
    rm_spec = pl.BlockSpec((tq, DA_WIDTH), lambda i: (i, 0))
    vec_spec = pl.BlockSpec((1, LANES), lambda i: (0, 0))
    return pl.pallas_call(
        kern,
        out_shape=(row_major, row_major, hm, hm, hm),
        grid=(n // tq,),
        in_specs=[pl.BlockSpec((3 * DA_HEADS, tq, LANES), lambda i: (0, i, 0)), vec_spec, vec_spec, vec_spec],
        out_specs=(rm_spec, rm_spec, hm_spec, hm_spec, hm_spec),
        compiler_params=_cparams(("parallel",)),
        name="qkv_prep",
    )(proj, gq, gk, invf)


def _attn_kernel(lam_ref, q_ref, k_ref, v_ref, g_ref, sg_ref, o_ref, m_ref, l_ref, acc_ref,
                 *, tq, tk, q_pos0, kv_len, lam_init):
    qi = pl.program_id(2)
    q = q_ref[...]
    lane_q = lax.broadcasted_iota(jnp.int32, (tq, LANES), 1)
    lo_q = lane_q < DA_QK_DIM
    zero = jnp.zeros_like(q)
    q_maps = (jnp.where(lo_q, q, zero), jnp.where(lo_q, zero, q))

    m_ref[...] = jnp.full(m_ref.shape, NEG_INF, F32)
    l_ref[...] = jnp.zeros(l_ref.shape, F32)
    acc_ref[...] = jnp.zeros(acc_ref.shape, F32)

    def block(j, masked):
        start = pl.multiple_of(j * tk, tk)
        k = k_ref[pl.ds(start, tk), :]
        v = v_ref[pl.ds(start, tk), :]
        if masked:
            q_pos = q_pos0 + qi * tq + lax.broadcasted_iota(jnp.int32, (tq, tk), 0)
            k_pos = j * tk + lax.broadcasted_iota(jnp.int32, (tq, tk), 1)
            visible = (lax.shift_right_logical(k_pos, 6) <= lax.shift_right_logical(q_pos, 6)) & (k_pos < kv_len)
        for c in range(2):
            s = lax.dot_general(q_maps[c], k, (((1,), (1,)), ((), ())), preferred_element_type=F32)
            if masked:
                s = jnp.where(visible, s, NEG_INF)
            m_prev = m_ref[c]
            m_new = jnp.maximum(m_prev, jnp.max(s, axis=-1, keepdims=True))
            alpha = jnp.exp(m_prev - m_new)
            p = jnp.exp(s - m_new)
            l_ref[c] = alpha * l_ref[c] + jnp.sum(p, axis=-1, keepdims=True)
            acc_ref[c] = alpha * acc_ref[c] + jnp.dot(p.astype(BF16), v, preferred_element_type=F32)
            m_ref[c] = m_new

    n_full = (q_pos0 + qi * tq) // tk

    def full_body(j, carry):
        block(j, False)
        return carry

    lax.fori_loop(0, n_full, full_body, 0)
    block(n_full, True)

    lam4 = lam_ref[...]
    lam = (jnp.exp(jnp.sum(lam4[0:1] * lam4[1:2], axis=-1, keepdims=True))
           - jnp.exp(jnp.sum(lam4[2:3] * lam4[3:4], axis=-1, keepdims=True)) + lam_init)
    o = acc_ref[0] / l_ref[0] - lam * (acc_ref[1] / l_ref[1])
    ms = jnp.mean(o * o, axis=-1, keepdims=True)
    o = o * lax.rsqrt(ms + NORM_EPS) * sg_ref[...] * (1.0 - lam_init)
    g = g_ref[...]
    o_ref[...] = (o * (g * jax.nn.sigmoid(g))).astype(o_ref.dtype)


def _attention(lam4, q_bf, k_bf, v_bf, proj, subln_g, *, batch, t_q, t_k, tq, tk, q_pos0, kv_len, lam_init):
    nq = t_q // tq
    assert t_k % tk == 0 and q_pos0 + t_q <= t_k and tq % CHUNK == 0 or nq == 1
    kern = functools.partial(_attn_kernel, tq=tq, tk=tk, q_pos0=q_pos0, kv_len=kv_len, lam_init=lam_init)
    return pl.pallas_call(
        kern,
        out_shape=jax.ShapeDtypeStruct((batch * t_q, DA_WIDTH), BF16),
        grid=(batch, DA_HEADS, nq),
        in_specs=[
            pl.BlockSpec((4, DA_QK_DIM), lambda b, h, i: (0, 0)),
            pl.BlockSpec((None, tq, LANES), lambda b, h, i: (h, b * nq + i, 0)),
            pl.BlockSpec((None, None, t_k, LANES), lambda b, h, i: (h, b, 0, 0)),
            pl.BlockSpec((None, None, t_k, LANES), lambda b, h, i: (h, b, 0, 0)),
            pl.BlockSpec((None, tq, LANES), lambda b, h, i: (GAB0 + h, b * nq + i, 0)),
            pl.BlockSpec((1, LANES), lambda b, h, i: (0, 0)),
        ],
        out_specs=pl.BlockSpec((tq, LANES), lambda b, h, i: (b * nq + i, h)),
        scratch_shapes=[pltpu.VMEM((2, tq, 1), F32), pltpu.VMEM((2, tq, 1), F32), pltpu.VMEM((2, tq, LANES), F32)],
        compiler_params=_cparams(("parallel", "parallel", "arbitrary")),
        name="diff_attention",
    )(lam4, q_bf, k_bf, v_bf, proj, subln_g)


def _mm(a, b):
    return jnp.dot(a, b, preferred_element_type=F32, precision=HI)


def _mm_nt(a, b):
    return lax.dot_general(a, b, (((1,), (1,)), ((), ())), preferred_element_type=F32, precision=HI)


def _rwkv_kernel(r_ref, k_ref, v_ref, g_ref, wa_ref, shift0_ref, wkv0_ref, mu_ref, w0_ref, a0_ref, kk_ref,
                 ka_ref, rk_ref, lng_ref, lnb_ref, wup_ref, aup_ref,
                 out_ref, state_ref, shift_ref,
                 h_ref, prev_ref, y_ref, *, c_len):
    c = pl.program_id(1)
    nc = pl.num_programs(1)

    @pl.when(c == 0)
    def _():
        h_ref[...] = wkv0_ref[0]
        prev_ref[...] = shift0_ref[0]

    row = lax.broadcasted_iota(jnp.int32, (c_len, LANES), 0)
    lane = lax.broadcasted_iota(jnp.int32, (c_len, LANES), 1)
    lo = lane < RW_HEAD
    rr = lax.broadcasted_iota(jnp.int32, (c_len, c_len), 0)
    cc = lax.broadcasted_iota(jnp.int32, (c_len, c_len), 1)
    tril_incl = (cc <= rr).astype(F32)
    tril_strict = (cc < rr).astype(F32)
    er = lax.broadcasted_iota(jnp.int32, (RW_HEAD, RW_HEAD), 0)
    ec = lax.broadcasted_iota(jnp.int32, (RW_HEAD, RW_HEAD), 1)
    eye = (er == ec).astype(F32)
    n_double = int(math.log2(c_len))

    def token_shift(x, blk):
        prev = jnp.where(row == 0, prev_ref[blk], pltpu.roll(x, 1, 0))
        return x + mu_ref[blk] * (prev - x)

    wa_raw = wa_ref[0]
    xs_wa = token_shift(wa_raw, 3 * RW_BLOCKS)
    tanh_wa = jnp.tanh(xs_wa)

    def per_block(p, carry):
        r_raw = r_ref[p]
        k_raw = k_ref[p]
        v_raw = v_ref[p]
        xr = token_shift(r_raw, p)
        xk = token_shift(k_raw, RW_BLOCKS + p)
        xv = token_shift(v_raw, 2 * RW_BLOCKS + p)

        w_lin = w0_ref[p] + _mm(tanh_wa, wup_ref[p])
        a_lin = a0_ref[p] + _mm(xs_wa, aup_ref[p])
        neg = -w_lin
        softplus = jnp.maximum(neg, 0.0) + jnp.log(1.0 + jnp.exp(-jnp.abs(neg)))
        log_decay = -jnp.exp(-softplus - 0.5)
        a_sig = jax.nn.sigmoid(a_lin)
        kk_raw = xk * kk_ref[p]
        kk = kk_raw / jnp.maximum(jnp.sqrt(_group_sums(kk_raw * kk_raw, lo)), 1e-12)
        k_mod = xk * (1.0 + (a_sig - 1.0) * ka_ref[p])
        a_vec = -kk
        b_vec = kk * a_sig

        cs = _mm(tril_incl, log_decay)
        cs_last = cs[c_len - 1:c_len, :]
        decay_in = jnp.exp(cs)
        decay_out = jnp.exp(-cs)
        decay_to_end = jnp.exp(cs_last - cs)
        a_t = a_vec * jnp.exp(cs - log_decay)
        b_t = b_vec * decay_out
        k_t = k_mod * decay_out
        r_t = xr * decay_in
        b_p = b_vec * decay_to_end
        k_p = k_mod * decay_to_end
        p_c = jnp.exp(cs_last)

        ys = []
        for hh in range(2):
            sl = slice(hh * RW_HEAD, (hh + 1) * RW_HEAD)
            at, bt, kt, rt, bp, kp, vv = a_t[:, sl], b_t[:, sl], k_t[:, sl], r_t[:, sl], b_p[:, sl], k_p[:, sl], xv[:, sl]
            a_ab = tril_strict * _mm_nt(at, bt)
            a_ak = tril_strict * _mm_nt(at, kt)
            a_rb = tril_incl * _mm_nt(rt, bt)
            a_rk = tril_incl * _mm_nt(rt, kt)
            x_w = at
            x_u = _mm(a_ak, vv)
            a_pow = a_ab
            for step in range(n_double):
                x_w = x_w + _mm(a_pow, x_w)
                x_u = x_u + _mm(a_pow, x_u)
                if step + 1 < n_double:
                    a_pow = _mm(a_pow, a_pow)
            bp_t = bp.T
            kp_t = kp.T
            m_mat = eye * p_c[:, sl] + _mm(bp_t, x_w)
            n_mat = _mm(bp_t, x_u) + _mm(kp_t, vv)
            g_mat = rt + _mm(a_rb, x_w)
            y0 = _mm(a_rb, x_u) + _mm(a_rk, vv)
            hd = 2 * p + hh
            h_state = h_ref[hd]
            ys.append(_mm(g_mat, h_state) + y0)
            h_ref[hd] = _mm(m_mat, h_state) + n_mat
        y = jnp.concatenate(ys, axis=1)

        inv_n = 1.0 / RW_HEAD
        mean = _group_sums(y, lo) * inv_n
        yc = y - mean
        var = _group_sums(yc * yc, lo) * inv_n
        yn = yc * lax.rsqrt(var + LNX_EPS) * lng_ref[p] + lnb_ref[p]
        bonus = _group_sums(xr * k_mod * rk_ref[p], lo) * xv
        g = g_ref[p]
        y_ref[p] = ((yn + bonus) * (g * jax.nn.sigmoid(g))).astype(y_ref.dtype)

        prev_ref[p] = r_raw[c_len - 1:c_len, :]
        prev_ref[RW_BLOCKS + p] = k_raw[c_len - 1:c_len, :]
        prev_ref[2 * RW_BLOCKS + p] = v_raw[c_len - 1:c_len, :]
        return carry

    lax.fori_loop(0, RW_BLOCKS, per_block, 0)
    prev_ref[3 * RW_BLOCKS] = wa_raw[c_len - 1:c_len, :]

    for p in range(RW_BLOCKS):
        out_ref[:, p * LANES:(p + 1) * LANES] = y_ref[p]

    @pl.when(c == nc - 1)
    def _():
        state_ref[0] = h_ref[...]
        shift_ref[0] = prev_ref[...]


def _rwkv(proj, shift0_b, wkv0_t, prm, *, batch, t_len, c_len):
    nc = t_len // c_len
    n = batch * t_len
    kern = functools.partial(_rwkv_kernel, c_len=c_len)
    blk8 = lambda base: pl.BlockSpec((RW_BLOCKS, c_len, LANES), lambda b, c: (base // RW_BLOCKS, b * nc + c, 0))
    vec8 = pl.BlockSpec((RW_BLOCKS, 1, LANES), lambda b, c: (0, 0, 0))
    lora = pl.BlockSpec((RW_BLOCKS, LANES, LANES), lambda b, c: (0, 0, 0))
    return pl.pallas_call(
        kern,
        out_shape=(
            jax.ShapeDtypeStruct((n, RW_WIDTH), BF16),
            jax.ShapeDtypeStruct((batch, RW_HEADS, RW_HEAD, RW_HEAD), F32),
            jax.ShapeDtypeStruct((batch, SHIFT_BLOCKS, 1, LANES), F32),
        ),
        grid=(batch, nc),
        in_specs=[
            blk8(RB0), blk8(KRB0), blk8(VRB0), blk8(GRB0),
            pl.BlockSpec((1, c_len, LANES), lambda b, c: (WAB0, b * nc + c, 0)),
            pl.BlockSpec((1, SHIFT_BLOCKS, 1, LANES), lambda b, c: (b, 0, 0, 0)),
            pl.BlockSpec((1, RW_HEADS, RW_HEAD, RW_HEAD), lambda b, c: (b, 0, 0, 0)),
            pl.BlockSpec((SHIFT_BLOCKS, 1, LANES), lambda b, c: (0, 0, 0)),
            vec8, vec8, vec8, vec8, vec8, vec8, vec8, lora, lora,
        ],
        out_specs=(
            pl.BlockSpec((c_len, RW_WIDTH), lambda b, c: (b * nc + c, 0)),
            pl.BlockSpec((1, RW_HEADS, RW_HEAD, RW_HEAD), lambda b, c: (b, 0, 0, 0)),
            pl.BlockSpec((1, SHIFT_BLOCKS, 1, LANES), lambda b, c: (b, 0, 0, 0)),
        ),
        scratch_shapes=[
            pltpu.VMEM((RW_HEADS, RW_HEAD, RW_HEAD), F32),
            pltpu.VMEM((SHIFT_BLOCKS, 1, LANES), F32),
            pltpu.VMEM((RW_BLOCKS, c_len, LANES), BF16),
        ],
        compiler_params=_cparams(("parallel", "arbitrary")),
        name="rwkv7_chunked",
    )(proj, proj, proj, proj, proj, shift0_b, wkv0_t, prm["mu"], prm["w0"], prm["a0"], prm["k_k"], prm["k_a"],
      prm["r_k"], prm["lnx_g"], prm["lnx_b"], prm["w_up"], prm["a_up"])


def _out_proj_kernel(att_ref, rw_ref, wa_ref, wr_ref, x_ref, o_ref):
    acc = jnp.dot(att_ref[...], wa_ref[...], preferred_element_type=F32)
    acc = acc + jnp.dot(rw_ref[...], wr_ref[...], preferred_element_type=F32)
    o_ref[...] = x_ref[...] + acc


def _out_proj(att, rw, w_att, w_rw, x2, *, tm):
    n = x2.shape[0]
    return pl.pallas_call(
        _out_proj_kernel,
        out_shape=jax.ShapeDtypeStruct((n, D_MODEL), F32),
        grid=(n // tm,),
        in_specs=[
            pl.BlockSpec((tm, DA_WIDTH), lambda i: (i, 0)),
            pl.BlockSpec((tm, RW_WIDTH), lambda i: (i, 0)),
            pl.BlockSpec((DA_WIDTH, D_MODEL), lambda i: (0, 0)),
            pl.BlockSpec((RW_WIDTH, D_MODEL), lambda i: (0, 0)),
            pl.BlockSpec((tm, D_MODEL), lambda i: (i, 0)),
        ],
        out_specs=pl.BlockSpec((tm, D_MODEL), lambda i: (i, 0)),
        compiler_params=_cparams(("parallel",)),
        name="out_proj",
    )(att, rw, w_att, w_rw, x2)


def _blocks(v):
    return v.reshape(-1, 1, LANES)


def _prepare_params(norm_g, w_in, q_norm_g, k_norm_g, lq1, lk1, lq2, lk2, subln_g, shift_mu, w0, w_up, a0, a_up,
                    k_k, k_a, r_k, lnx_g, lnx_b, w_out):
    rw0 = 4 * DA_WIDTH
    lora0 = rw0 + 3 * RW_WIDTH
    g0 = rw0 + RW_SHIFT_COLS
    w_perm = jnp.concatenate([w_in[:, :lora0], w_in[:, g0:], w_in[:, lora0:g0]], axis=1)
    half = DA_QK_DIM // 2
    inv_freq = ROPE_THETA ** (-jnp.arange(half, dtype=F32) / half)
    zeros = jnp.zeros((RW_LORA, RW_WIDTH), F32)
    lora_blocks = lambda m: m.reshape(LANES, RW_BLOCKS, LANES).transpose(1, 0, 2)
    return dict(
        norm_g=norm_g.reshape(1, D_MODEL),
        w_in=w_perm.astype(BF16),
        gq=jnp.tile(q_norm_g, 2).reshape(1, LANES),
        gk=jnp.tile(k_norm_g, 2).reshape(1, LANES),
        invf=jnp.tile(inv_freq, 4).reshape(1, LANES),
        lam4=jnp.stack([lq1, lk1, lq2, lk2]),
        subln_g=subln_g.reshape(1, LANES),
        mu=_blocks(shift_mu),
        w0=_blocks(w0), a0=_blocks(a0), k_k=_blocks(k_k), k_a=_blocks(k_a), r_k=_blocks(r_k.reshape(-1)),
        lnx_g=_blocks(lnx_g), lnx_b=_blocks(lnx_b),
        w_up=lora_blocks(jnp.concatenate([w_up, zeros], axis=0)),
        a_up=lora_blocks(jnp.concatenate([zeros, a_up], axis=0)),
        w_att=w_out[:DA_WIDTH].astype(BF16),
        w_rw=w_out[DA_WIDTH:].astype(BF16),
    )


def _tiles(batch, t_len, past_len):
    n = batch * t_len
    row_tile = min(n, 1024)
    if past_len == 0:
        tq = tk = min(t_len, 512)
        t_k = t_len
    else:
        tq = t_len
        t_k = tk = -(-(past_len + t_len) // LANES) * LANES
    return dict(tm=row_tile, bn=5, tq_prep=min(n, 512), tq=tq, tk=tk, t_k=t_k,
                c_len=min(t_len, CHUNK), tm_out=min(n, 512))


def _layer(x, past_k, past_v, wkv0, shift0, layer_idx, prm):
    batch, t_len, _ = x.shape
    past_len = past_k.shape[1]
    n = batch * t_len
    assert t_len & (t_len - 1) == 0
    tl = _tiles(batch, t_len, past_len)
    lam_init = 0.8 - 0.6 * math.exp(-0.3 * layer_idx)

    x2 = x.reshape(n, D_MODEL)
    proj = _in_proj(x2, prm["norm_g"], prm["w_in"], tm=tl["tm"], bn=tl["bn"])
    k_out, v_out, q_bf, k_bf, v_bf = _qkv_prep(proj, prm["gq"], prm["gk"], prm["invf"], t_len=t_len,
                                               pos0=past_len, tq=tl["tq_prep"])

    head_major = lambda a: a.reshape(DA_HEADS, batch, t_len, LANES)
    k_hm, v_hm = head_major(k_bf), head_major(v_bf)
    if past_len:
        cache_hm = lambda c: c.reshape(batch, past_len, DA_HEADS, LANES).transpose(2, 0, 1, 3).astype(BF16)
        pad = jnp.zeros((DA_HEADS, batch, tl["t_k"] - past_len - t_len, LANES), BF16)
        k_hm = jnp.concatenate([cache_hm(past_k), k_hm, pad], axis=2)
        v_hm = jnp.concatenate([cache_hm(past_v), v_hm, pad], axis=2)

    att = _attention(prm["lam4"], q_bf, k_hm, v_hm, proj, prm["subln_g"], batch=batch, t_q=t_len, t_k=tl["t_k"],
                     tq=tl["tq"], tk=tl["tk"], q_pos0=past_len, kv_len=past_len + t_len, lam_init=lam_init)

    shift0_b = shift0.reshape(batch, SHIFT_BLOCKS, 1, LANES)
    wkv0_t = jnp.swapaxes(wkv0, -1, -2)
    rw, state_t, shift_b = _rwkv(proj, shift0_b, wkv0_t, prm, batch=batch, t_len=t_len, c_len=tl["c_len"])

    y = _out_proj(att, rw, prm["w_att"], prm["w_rw"], x2, tm=tl["tm_out"])
    return (y.reshape(batch, t_len, D_MODEL),
            k_out.reshape(batch, t_len, DA_HEADS, 2, DA_QK_DIM),
            v_out.reshape(batch, t_len, DA_HEADS, DA_V_DIM),
            jnp.swapaxes(state_t, -1, -2),
            shift_b.reshape(batch, 1, RW_SHIFT_COLS))


def kernel(x_prompt, x_sample, cache_attn_k, cache_attn_v, state_rwkv_wkv, state_rwkv_shift, norm_g, w_in, q_norm_g, k_norm_g, lambda_q1, lambda_k1, lambda_q2, lambda_k2, subln_g, shift_mu, w0, w_up, a0, a_up, k_k, k_a, r_k, lnx_g, lnx_b, w_out):
    depth = w_in.shape[0]
    batch = x_prompt.shape[0]
    dt = x_prompt.dtype
    yp, ys = x_prompt, x_sample
    outs = [[] for _ in range(8)]
    for l in range(depth):
        prm = _prepare_params(norm_g[l], w_in[l], q_norm_g[l], k_norm_g[l], lambda_q1[l], lambda_k1[l], lambda_q2[l],
                              lambda_k2[l], subln_g[l], shift_mu[l], w0[l], w_up[l], a0[l], a_up[l], k_k[l], k_a[l],
                              r_k[l], lnx_g[l], lnx_b[l], w_out[l])
        yp, kp, vp, sp, hp = _layer(
            yp, jnp.zeros((batch, 0, DA_HEADS, 2, DA_QK_DIM), dt), jnp.zeros((batch, 0, DA_HEADS, DA_V_DIM), dt),
            jnp.zeros((batch, RW_HEADS, RW_HEAD, RW_HEAD), dt), jnp.zeros((batch, 1, RW_SHIFT_COLS), dt), l, prm)
        ys, ksn, vsn, ssn, hsn = _layer(ys, cache_attn_k[l], cache_attn_v[l], state_rwkv_wkv[l],
                                        state_rwkv_shift[l], l, prm)
        for lst, val in zip(outs, (kp, vp, sp, hp, ksn, vsn, ssn, hsn)):
            lst.append(val)
    return (yp, ys) + tuple(jnp.stack(o) for o in outs)
```

```python
import functools
import math

import jax
import jax.numpy as jnp
from jax import lax
from jax.experimental import pallas as pl
from jax.experimental.pallas import tpu as pltpu

F32 = jnp.float32
BF16 = jnp.bfloat16

D_MODEL = 2048
CHUNK = 64
DA_HEADS = 8
DA_V_DIM = 128
DA_QK_DIM = 64
DA_WIDTH = DA_HEADS * DA_V_DIM
RW_WIDTH = 1024
RW_HEAD = 64
RW_HEADS = RW_WIDTH // RW_HEAD
RW_LORA = 64
RW_SHIFT_COLS = 3 * RW_WIDTH + 2 * RW_LORA
IN_COLS = 4 * DA_WIDTH + RW_SHIFT_COLS + RW_WIDTH
ROPE_THETA = 10000.0
NORM_EPS = 1e-6
LNX_EPS = 64e-5
NEG_INF = -1e30
LOG2_E = math.log2(math.e)

LANES = 128
N_COL_BLOCKS = IN_COLS // LANES
QB0, KB0, VB0, GAB0, RB0, KRB0, VRB0, GRB0, WAB0 = 0, 8, 16, 24, 32, 40, 48, 56, 64
RW_BLOCKS = RW_WIDTH // LANES
SHIFT_BLOCKS = RW_SHIFT_COLS // LANES

VMEM_LIMIT = 56 * 1024 * 1024

NT_DIMS = (((1,), (1,)), ((), ()))


def _cparams(sem):
    return pltpu.CompilerParams(dimension_semantics=sem, vmem_limit_bytes=VMEM_LIMIT)


def _bdot(a, b):
    return jnp.dot(a.astype(BF16), b.astype(BF16), preferred_element_type=F32)


def _split2(x):
    hi = x.astype(BF16)
    return hi, (x - hi.astype(F32)).astype(BF16)


def _group_sums(x, lo):
    s_lo = jnp.sum(jnp.where(lo, x, 0.0), axis=-1, keepdims=True)
    s_hi = jnp.sum(jnp.where(lo, 0.0, x), axis=-1, keepdims=True)
    return jnp.where(lo, s_lo, s_hi)


def _silu(g):
    return g * jax.nn.sigmoid(g)


def _in_proj_kernel(x_ref, g_ref, w_ref, o_ref, xn_ref):
    @pl.when(pl.program_id(1) == 0)
    def _():
        xf = x_ref[...]
        ms = jnp.mean(xf * xf, axis=-1, keepdims=True)
        xn_ref[...] = (xf * lax.rsqrt(ms + NORM_EPS) * g_ref[...]).astype(BF16)

    acc = jnp.dot(xn_ref[...], w_ref[...], preferred_element_type=F32)
    for c in range(o_ref.shape[0]):
        o_ref[c] = acc[:, c * LANES:(c + 1) * LANES]


def _in_proj(x2, norm_g, w_bf, *, tm, bn):
    n = x2.shape[0]
    return pl.pallas_call(
        _in_proj_kernel,
        out_shape=jax.ShapeDtypeStruct((N_COL_BLOCKS, n, LANES), F32),
        grid=(n // tm, N_COL_BLOCKS // bn),
        in_specs=[
            pl.BlockSpec((tm, D_MODEL), lambda i, j: (i, 0)),
            pl.BlockSpec((1, D_MODEL), lambda i, j: (0, 0)),
            pl.BlockSpec((D_MODEL, bn * LANES), lambda i, j: (0, j)),
        ],
        out_specs=pl.BlockSpec((bn, tm, LANES), lambda i, j: (j, i, 0)),
        scratch_shapes=[pltpu.VMEM((tm, D_MODEL), BF16)],
        compiler_params=_cparams(("parallel", "arbitrary")),
        name="in_proj",
    )(x2, norm_g, w_bf)


def _qkv_kernel(p_ref, gq_ref, gk_ref, invf_ref, kout_ref, vout_ref, qs_ref, kbf_ref, vt_ref, *, tq, tv, pos0):
    ti = pl.program_id(1)
    row = lax.broadcasted_iota(jnp.int32, (tq, LANES), 0)
    lane = lax.broadcasted_iota(jnp.int32, (tq, LANES), 1)
    ang = (pos0 + ti * tq + row).astype(F32) * invf_ref[...]
    cos = jnp.cos(ang)
    sin = jnp.sin(ang)
    first_half = jnp.bitwise_and(lane, DA_QK_DIM - 1) < DA_QK_DIM // 2
    sin_signed = jnp.where(first_half, -sin, sin)
    lo = lane < DA_QK_DIM

    def norm_rope(x, g):
        ms = _group_sums(x * x, lo) * (1.0 / DA_QK_DIM)
        xn = x * lax.rsqrt(ms + NORM_EPS) * g
        rot = jnp.where(first_half, pltpu.roll(xn, LANES - 32, 1), pltpu.roll(xn, 32, 1))
        return xn * cos + rot * sin_signed

    gq = gq_ref[...]
    gk = gk_ref[...]
    for h in range(DA_HEADS):
        q = norm_rope(p_ref[QB0 + h], gq) * (LOG2_E * DA_QK_DIM ** -0.5)
        qs_ref[h, 0:tq, :] = jnp.where(lo, q, 0.0).astype(BF16)
        qs_ref[h, tq:2 * tq, :] = jnp.where(lo, 0.0, q).astype(BF16)
        k = norm_rope(p_ref[KB0 + h], gk)
        kout_ref[:, h * LANES:(h + 1) * LANES] = k
        kbf_ref[h] = k.astype(BF16)
        v = p_ref[VB0 + h]
        vout_ref[:, h * LANES:(h + 1) * LANES] = v
        for s in range(tq // tv):
            vt_ref[h, s] = v[s * tv:(s + 1) * tv, :].T.astype(BF16)


def _qkv_prep(proj, gq, gk, invf, *, batch, t_len, pos0, tq, tv):
    n = batch * t_len
    nq = t_len // tq
    kern = functools.partial(_qkv_kernel, tq=tq, tv=tv, pos0=pos0)
    row_major = jax.ShapeDtypeStruct((n, DA_WIDTH), F32)
    rm_spec = pl.BlockSpec((tq, DA_WIDTH), lambda b, t: (b * nq + t, 0))
    vec_spec = pl.BlockSpec((1, LANES), lambda b, t: (0, 0))
    return pl.pallas_call(
        kern,
        out_shape=(row_major, row_major,
                   jax.ShapeDtypeStruct((DA_HEADS, 2 * n, LANES), BF16),
                   jax.ShapeDtypeStruct((DA_HEADS, n, LANES), BF16),
                   jax.ShapeDtypeStruct((DA_HEADS, batch, t_len // tv, LANES, tv), BF16)),
        grid=(batch, nq),
        in_specs=[pl.BlockSpec((3 * DA_HEADS, tq, LANES), lambda b, t: (0, b * nq + t, 0)),
                  vec_spec, vec_spec, vec_spec],
        out_specs=(rm_spec, rm_spec,
                   pl.BlockSpec((DA_HEADS, 2 * tq, LANES), lambda b, t: (0, b * nq + t, 0)),
                   pl.BlockSpec((DA_HEADS, tq, LANES), lambda b, t: (0, b * nq + t, 0)),
                   pl.BlockSpec((DA_HEADS, None, tq // tv, LANES, tv), lambda b, t: (0, b, t, 0, 0))),
        compiler_params=_cparams(("parallel", "parallel")),
        name="qkv_prep",
    )(proj, gq, gk, invf)


def _attn_kernel(lam_ref, qs_ref, k_ref, vt_ref, g_ref, sg_ref, o_ref, m_ref, l_ref, acc_ref, sta_ref, stb_ref,
                 *, tq, tk, n_tail, q_pos0, kv_len, lam_init):
    qi = pl.program_id(2)
    qs = qs_ref[...]

    m_ref[...] = jnp.full(m_ref.shape, NEG_INF, F32)
    l_ref[...] = jnp.zeros(l_ref.shape, F32)
    acc_ref[...] = jnp.zeros(acc_ref.shape, F32)

    def scores_into(st_ref, j):
        k = k_ref[pl.ds(pl.multiple_of(j * tk, tk), tk), :]
        st_ref[...] = lax.dot_general(k, qs, NT_DIMS, preferred_element_type=F32)

    def consume(st_ref, j, masked):
        st = st_ref[...]
        if masked:
            k_pos = j * tk + lax.broadcasted_iota(jnp.int32, (tk, 2 * tq), 0)
            col = lax.broadcasted_iota(jnp.int32, (tk, 2 * tq), 1)
            q_pos = q_pos0 + qi * tq + jnp.bitwise_and(col, tq - 1)
            visible = (lax.shift_right_logical(k_pos, 6) <= lax.shift_right_logical(q_pos, 6)) & (k_pos < kv_len)
            st = jnp.where(visible, st, NEG_INF)
        m_prev = m_ref[...]
        m_new = jnp.maximum(m_prev, jnp.max(st, axis=0, keepdims=True))
        alpha = jnp.exp2(m_prev - m_new)
        p = jnp.exp2(st - m_new)
        l_ref[...] = alpha * l_ref[...] + jnp.sum(p, axis=0, keepdims=True)
        pv = jnp.dot(vt_ref[j], p.astype(BF16), preferred_element_type=F32)
        acc_ref[...] = alpha * acc_ref[...] + pv
        m_ref[...] = m_new

    n_full = (q_pos0 + qi * tq) // tk
    scores_into(sta_ref, 0)

    def pair_body(i, carry):
        j = 2 * i
        scores_into(stb_ref, j + 1)
        consume(sta_ref, j, False)
        scores_into(sta_ref, j + 2)
        consume(stb_ref, j + 1, False)
        return carry

    lax.fori_loop(0, n_full // 2, pair_body, 0)
    bufs = (sta_ref, stb_ref)
    for t in range(n_tail):
        if t + 1 < n_tail:
            scores_into(bufs[(t + 1) % 2], n_full + t + 1)
        consume(bufs[t % 2], n_full + t, True)

    lam4 = lam_ref[...]
    lam = (jnp.exp(jnp.sum(lam4[0:1] * lam4[1:2], axis=-1, keepdims=True))
           - jnp.exp(jnp.sum(lam4[2:3] * lam4[3:4], axis=-1, keepdims=True)) + lam_init)
    on = acc_ref[...] / l_ref[...]
    ot = on[:, 0:tq] - lam * on[:, tq:2 * tq]
    ms = jnp.mean(ot * ot, axis=0, keepdims=True)
    ot = ot * lax.rsqrt(ms + NORM_EPS) * sg_ref[...] * (1.0 - lam_init)
    o_ref[...] = (ot.T * _silu(g_ref[...])).astype(o_ref.dtype)


def _attention(lam4, qs_bf, k_bf, vt_bf, proj, subln_col, *, batch, t_q, t_k, tq, tk, q_pos0, kv_len, lam_init):
    nq = t_q // tq
    nkb = t_k // tk
    n_tail = max(1, tq // tk)
    assert t_k % tk == 0 and q_pos0 + t_q <= t_k and tq & (tq - 1) == 0
    assert (tq == 2 * tk and q_pos0 == 0) or (nq == 1 and nkb == 1)
    kern = functools.partial(_attn_kernel, tq=tq, tk=tk, n_tail=n_tail, q_pos0=q_pos0, kv_len=kv_len,
                             lam_init=lam_init)
    return pl.pallas_call(
        kern,
        out_shape=jax.ShapeDtypeStruct((batch * t_q, DA_WIDTH), BF16),
        grid=(batch, DA_HEADS, nq),
        in_specs=[
            pl.BlockSpec((4, DA_QK_DIM), lambda b, h, i: (0, 0)),
            pl.BlockSpec((None, 2 * tq, LANES), lambda b, h, i: (h, b * nq + i, 0)),
            pl.BlockSpec((None, None, t_k, LANES), lambda b, h, i: (h, b, 0, 0)),
            pl.BlockSpec((None, None, nkb, LANES, tk), lambda b, h, i: (h, b, 0, 0, 0)),
            pl.BlockSpec((None, tq, LANES), lambda b, h, i: (GAB0 + h, b * nq + i, 0)),
            pl.BlockSpec((LANES, 1), lambda b, h, i: (0, 0)),
        ],
        out_specs=pl.BlockSpec((tq, LANES), lambda b, h, i: (b * nq + i, h)),
        scratch_shapes=[pltpu.VMEM((1, 2 * tq), F32), pltpu.VMEM((1, 2 * tq), F32),
                        pltpu.VMEM((LANES, 2 * tq), F32),
                        pltpu.VMEM((tk, 2 * tq), F32), pltpu.VMEM((tk, 2 * tq), F32)],
        compiler_params=_cparams(("parallel", "parallel", "arbitrary")),
        name="diff_attention",
    )(lam4, qs_bf, k_bf, vt_bf, proj, subln_col)


def _rwkv_kernel(r_ref, k_ref, v_ref, g_ref, wa_ref, shift0_ref, wkv0_ref, mu_ref, w0_ref, a0_ref, kk_ref,
                 ka_ref, rk_ref, lng_ref, lnb_ref, wup_ref, aup_ref,
                 out_ref, state_ref, shift_ref,
                 h_ref, prev_ref, *, c_len):
    c = pl.program_id(1)
    nc = pl.num_programs(1)

    @pl.when(c == 0)
    def _():
        h_ref[...] = wkv0_ref[0]
        prev_ref[...] = shift0_ref[0]

    cl = c_len
    row = lax.broadcasted_iota(jnp.int32, (cl, LANES), 0)
    lane = lax.broadcasted_iota(jnp.int32, (cl, LANES), 1)
    lo = lane < RW_HEAD
    own = (lo, jnp.logical_not(lo))
    sr = jnp.bitwise_and(lax.broadcasted_iota(jnp.int32, (2 * cl, 2 * cl), 0), cl - 1)
    sc = jnp.bitwise_and(lax.broadcasted_iota(jnp.int32, (2 * cl, 2 * cl), 1), cl - 1)
    top = lax.broadcasted_iota(jnp.int32, (2 * cl, 2 * cl), 0) < cl
    score_mask = sc < sr + jnp.where(top, 0, 1)
    tr = lax.broadcasted_iota(jnp.int32, (cl, cl), 0)
    tc = lax.broadcasted_iota(jnp.int32, (cl, cl), 1)
    tril_incl = (tc <= tr).astype(BF16)
    hr = lax.broadcasted_iota(jnp.int32, (RW_HEAD, LANES), 0)
    hl = lax.broadcasted_iota(jnp.int32, (RW_HEAD, LANES), 1)
    diag = (hl == hr, hl == hr + RW_HEAD)
    h_own = (hl < RW_HEAD, hl >= RW_HEAD)
    zeros_c = jnp.zeros((cl, LANES), F32)
    zeros_h = jnp.zeros((RW_HEAD, LANES), BF16)
    n_double = int(math.log2(cl))

    def token_shift(x, blk):
        prev = jnp.where(row == 0, prev_ref[blk], pltpu.roll(x, 1, 0))
        return x + mu_ref[blk] * (prev - x)

    wa_raw = wa_ref[0]
    xs_wa = token_shift(wa_raw, 3 * RW_BLOCKS)
    w_lora = _bdot(jnp.tanh(xs_wa), wup_ref[...])
    a_lora = _bdot(xs_wa, aup_ref[...])

    heads = range(RW_HEADS)
    blk = {}
    at_m, rt_m, v_m, scores = {}, {}, {}, {}
    for p in range(RW_BLOCKS):
        cols = slice(p * LANES, (p + 1) * LANES)
        r_raw = r_ref[p]
        k_raw = k_ref[p]
        v_raw = v_ref[p]
        xr = token_shift(r_raw, p)
        xk = token_shift(k_raw, RW_BLOCKS + p)
        xv = token_shift(v_raw, 2 * RW_BLOCKS + p)
        prev_ref[p] = r_raw[cl - 1:cl, :]
        prev_ref[RW_BLOCKS + p] = k_raw[cl - 1:cl, :]
        prev_ref[2 * RW_BLOCKS + p] = v_raw[cl - 1:cl, :]

        neg = -(w0_ref[p] + w_lora[:, cols])
        softplus = jnp.maximum(neg, 0.0) + jnp.log(1.0 + jnp.exp(-jnp.abs(neg)))
        log_decay = -jnp.exp(-softplus - 0.5)
        a_sig = jax.nn.sigmoid(a0_ref[p] + a_lora[:, cols])
        kk_raw = xk * kk_ref[p]
        kk = kk_raw / jnp.maximum(jnp.sqrt(_group_sums(kk_raw * kk_raw, lo)), 1e-12)
        k_mod = xk * (1.0 + (a_sig - 1.0) * ka_ref[p])
        b_vec = kk * a_sig

        ld_hi, ld_rest = log_decay.astype(BF16), None
        ld_rest = log_decay - ld_hi.astype(F32)
        ld_mid, ld_lo = _split2(ld_rest)
        cs = (jnp.dot(tril_incl, ld_hi, preferred_element_type=F32)
              + jnp.dot(tril_incl, ld_mid, preferred_element_type=F32)
              + jnp.dot(tril_incl, ld_lo, preferred_element_type=F32))
        cs_last = cs[cl - 1:cl, :]
        decay_out = jnp.exp(-cs)
        decay_to_end = jnp.exp(cs_last - cs)
        a_t = -kk * jnp.exp(cs - log_decay)
        r_t = xr * jnp.exp(cs)
        b_t = b_vec * decay_out
        k_t = k_mod * decay_out
        p_c = jnp.exp(cs_last)

        rhs_scores = jnp.concatenate([b_t, k_t], axis=0).astype(BF16)
        ends_t = jnp.concatenate([b_vec * decay_to_end, k_mod * decay_to_end], axis=0).T
        v_roll = pltpu.roll(xv, RW_HEAD, 1)
        blk[p] = dict(ends_t=ends_t, p_c=p_c, bonus=_group_sums(xr * k_mod * rk_ref[p], lo) * xv)
        for hh in range(2):
            hd = 2 * p + hh
            at_m[hd] = jnp.where(own[hh], a_t, 0.0)
            rt_m[hd] = jnp.where(own[hh], r_t, 0.0)
            v_m[hd] = jnp.where(own[1 - hh], v_roll, 0.0)
            lhs_scores = jnp.concatenate([at_m[hd], rt_m[hd]], axis=0).astype(BF16)
            s = lax.dot_general(lhs_scores, rhs_scores, NT_DIMS, preferred_element_type=F32)
            scores[hd] = jnp.where(score_mask, s, 0.0)

    x = {hd: at_m[hd] + _bdot(scores[hd][0:cl, :], jnp.concatenate([zeros_c, v_m[hd]], axis=0)) for hd in heads}
    a_pow = {hd: scores[hd][0:cl, 0:cl] for hd in heads}
    for step in range(n_double):
        for hd in heads:
            if step + 1 < n_double:
                res = _bdot(a_pow[hd], jnp.concatenate([x[hd], a_pow[hd]], axis=1))
                x[hd] = x[hd] + res[:, 0:LANES]
                a_pow[hd] = res[:, LANES:LANES + cl]
            else:
                x[hd] = x[hd] + _bdot(a_pow[hd], x[hd])

    fin = {}
    for hd in heads:
        p, hh = divmod(hd, 2)
        lhs_fin = jnp.concatenate([blk[p]["ends_t"][hh * RW_HEAD:(hh + 1) * RW_HEAD, :], scores[hd][cl:2 * cl, :]],
                                  axis=0)
        fin[hd] = _bdot(lhs_fin, jnp.concatenate([x[hd], v_m[hd]], axis=0))

    y_m = {}
    for hd in heads:
        p, hh = divmod(hd, 2)
        m_packed = fin[hd][0:RW_HEAD, :] + jnp.where(diag[hh], blk[p]["p_c"], 0.0)
        g_packed = fin[hd][RW_HEAD:RW_HEAD + cl, :] + rt_m[hd]
        h_hi, h_lo = _split2(h_ref[hd])
        pad = (lambda t: jnp.concatenate([t, zeros_h], axis=0)) if hh == 0 else (
            lambda t: jnp.concatenate([zeros_h, t], axis=0))
        h_hi, h_lo = pad(h_hi), pad(h_lo)
        y_m[hd] = jnp.dot(g_packed.astype(BF16), h_hi, preferred_element_type=F32) \
            + jnp.where(own[1 - hh], g_packed, 0.0)
        m_hi, m_lo = _split2(m_packed)
        h_new = (jnp.dot(m_hi, h_hi, preferred_element_type=F32) + jnp.dot(m_hi, h_lo, preferred_element_type=F32)
                 + jnp.dot(m_lo, h_hi, preferred_element_type=F32))
        h_ref[hd] = h_new + jnp.where(h_own[1 - hh], m_packed, 0.0)

    inv_n = 1.0 / RW_HEAD
    for p in range(RW_BLOCKS):
        y = pltpu.roll(y_m[2 * p] + y_m[2 * p + 1], RW_HEAD, 1)
        yc = y - _group_sums(y, lo) * inv_n
        var = _group_sums(yc * yc, lo) * inv_n
        yn = yc * lax.rsqrt(var + LNX_EPS) * lng_ref[p] + lnb_ref[p]
        out_ref[:, p * LANES:(p + 1) * LANES] = ((yn + blk[p]["bonus"]) * _silu(g_ref[p])).astype(out_ref.dtype)

    prev_ref[3 * RW_BLOCKS] = wa_raw[cl - 1:cl, :]

    @pl.when(c == nc - 1)
    def _():
        state_ref[0] = h_ref[...]
        shift_ref[0] = prev_ref[...]


def _rwkv(proj, shift0_b, wkv0_p, prm, *, batch, t_len, c_len):
    nc = t_len // c_len
    n = batch * t_len
    kern = functools.partial(_rwkv_kernel, c_len=c_len)
    blk8 = lambda base: pl.BlockSpec((RW_BLOCKS, c_len, LANES), lambda b, c: (base // RW_BLOCKS, b * nc + c, 0))
    vec8 = pl.BlockSpec((RW_BLOCKS, 1, LANES), lambda b, c: (0, 0, 0))
    lora = pl.BlockSpec((LANES, RW_WIDTH), lambda b, c: (0, 0))
    state_spec = pl.BlockSpec((1, RW_HEADS, RW_HEAD, LANES), lambda b, c: (b, 0, 0, 0))
    shift_spec = pl.BlockSpec((1, SHIFT_BLOCKS, 1, LANES), lambda b, c: (b, 0, 0, 0))
    return pl.pallas_call(
        kern,
        out_shape=(
            jax.ShapeDtypeStruct((n, RW_WIDTH), BF16),
            jax.ShapeDtypeStruct((batch, RW_HEADS, RW_HEAD, LANES), F32),
            jax.ShapeDtypeStruct((batch, SHIFT_BLOCKS, 1, LANES), F32),
        ),
        grid=(batch, nc),
        in_specs=[
            blk8(RB0), blk8(KRB0), blk8(VRB0), blk8(GRB0),
            pl.BlockSpec((1, c_len, LANES), lambda b, c: (WAB0, b * nc + c, 0)),
            shift_spec, state_spec,
            pl.BlockSpec((SHIFT_BLOCKS, 1, LANES), lambda b, c: (0, 0, 0)),
            vec8, vec8, vec8, vec8, vec8, vec8, vec8, lora, lora,
        ],
        out_specs=(pl.BlockSpec((c_len, RW_WIDTH), lambda b, c: (b * nc + c, 0)), state_spec, shift_spec),
        scratch_shapes=[
            pltpu.VMEM((RW_HEADS, RW_HEAD, LANES), F32),
            pltpu.VMEM((SHIFT_BLOCKS, 1, LANES), F32),
        ],
        compiler_params=_cparams(("parallel", "arbitrary")),
        name="rwkv7_chunked",
    )(proj, proj, proj, proj, proj, shift0_b, wkv0_p, prm["mu"], prm["w0"], prm["a0"], prm["k_k"], prm["k_a"],
      prm["r_k"], prm["lnx_g"], prm["lnx_b"], prm["w_up"], prm["a_up"])


def _pack_state(wkv):
    h = jnp.swapaxes(wkv, -1, -2)
    z = jnp.zeros_like(h)
    even = jnp.concatenate([z, h], axis=-1)
    odd = jnp.concatenate([h, z], axis=-1)
    is_even = (jnp.arange(RW_HEADS) % 2 == 0)[None, :, None, None]
    return jnp.where(is_even, even, odd)


def _unpack_state(hp):
    is_even = (jnp.arange(RW_HEADS) % 2 == 0)[None, :, None, None]
    h = jnp.where(is_even, hp[..., RW_HEAD:], hp[..., :RW_HEAD])
    return jnp.swapaxes(h, -1, -2)


def _out_proj_kernel(att_ref, rw_ref, wa_ref, wr_ref, x_ref, o_ref):
    acc = jnp.dot(att_ref[...], wa_ref[...], preferred_element_type=F32)
    acc = acc + jnp.dot(rw_ref[...], wr_ref[...], preferred_element_type=F32)
    o_ref[...] = x_ref[...] + acc


def _out_proj(att, rw, w_att, w_rw, x2, *, tm):
    n = x2.shape[0]
    return pl.pallas_call(
        _out_proj_kernel,
        out_shape=jax.ShapeDtypeStruct((n, D_MODEL), F32),
        grid=(n // tm,),
        in_specs=[
            pl.BlockSpec((tm, DA_WIDTH), lambda i: (i, 0)),
            pl.BlockSpec((tm, RW_WIDTH), lambda i: (i, 0)),
            pl.BlockSpec((DA_WIDTH, D_MODEL), lambda i: (0, 0)),
            pl.BlockSpec((RW_WIDTH, D_MODEL), lambda i: (0, 0)),
            pl.BlockSpec((tm, D_MODEL), lambda i: (i, 0)),
        ],
        out_specs=pl.BlockSpec((tm, D_MODEL), lambda i: (i, 0)),
        compiler_params=_cparams(("parallel",)),
        name="out_proj",
    )(att, rw, w_att, w_rw, x2)


def _blocks(v):
    return v.reshape(-1, 1, LANES)


def _prepare_params(norm_g, w_in, q_norm_g, k_norm_g, lq1, lk1, lq2, lk2, subln_g, shift_mu, w0, w_up, a0, a_up,
                    k_k, k_a, r_k, lnx_g, lnx_b, w_out):
    rw0 = 4 * DA_WIDTH
    lora0 = rw0 + 3 * RW_WIDTH
    g0 = rw0 + RW_SHIFT_COLS
    w_perm = jnp.concatenate([w_in[:, :lora0], w_in[:, g0:], w_in[:, lora0:g0]], axis=1)
    half = DA_QK_DIM // 2
    inv_freq = ROPE_THETA ** (-jnp.arange(half, dtype=F32) / half)
    zeros = jnp.zeros((RW_LORA, RW_WIDTH), F32)
    return dict(
        norm_g=norm_g.reshape(1, D_MODEL),
        w_in=w_perm.astype(BF16),
        gq=jnp.tile(q_norm_g, 2).reshape(1, LANES),
        gk=jnp.tile(k_norm_g, 2).reshape(1, LANES),
        invf=jnp.tile(inv_freq, 4).reshape(1, LANES),
        lam4=jnp.stack([lq1, lk1, lq2, lk2]),
        subln_col=subln_g.reshape(LANES, 1),
        mu=_blocks(shift_mu),
        w0=_blocks(w0), a0=_blocks(a0), k_k=_blocks(k_k), k_a=_blocks(k_a), r_k=_blocks(r_k.reshape(-1)),
        lnx_g=_blocks(lnx_g), lnx_b=_blocks(lnx_b),
        w_up=jnp.concatenate([w_up, zeros], axis=0).astype(BF16),
        a_up=jnp.concatenate([zeros, a_up], axis=0).astype(BF16),
        w_att=w_out[:DA_WIDTH].astype(BF16),
        w_rw=w_out[DA_WIDTH:].astype(BF16),
    )


def _tiles(batch, t_len, past_len):
    n = batch * t_len
    if past_len == 0:
        tq = min(t_len, 512)
        tk = tv = tq // 2
        t_k = t_len
    else:
        tq = tv = t_len
        t_k = tk = -(-(past_len + t_len) // LANES) * LANES
    return dict(tm=min(n, 1024), bn=5, tq=tq, tk=tk, tv=tv, t_k=t_k, c_len=min(t_len, CHUNK), tm_out=min(n, 512))


def _layer(x, past_k, past_v, wkv0, shift0, layer_idx, prm):
    batch, t_len, _ = x.shape
    past_len = past_k.shape[1]
    n = batch * t_len
    assert t_len & (t_len - 1) == 0
    tl = _tiles(batch, t_len, past_len)
    tq, tk, t_k = tl["tq"], tl["tk"], tl["t_k"]
    lam_init = 0.8 - 0.6 * math.exp(-0.3 * layer_idx)

    x2 = x.reshape(n, D_MODEL)
    proj = _in_proj(x2, prm["norm_g"], prm["w_in"], tm=tl["tm"], bn=tl["bn"])
    k_out, v_out, qs_bf, k_bf, vt_bf = _qkv_prep(proj, prm["gq"], prm["gk"], prm["invf"], batch=batch, t_len=t_len,
                                                 pos0=past_len, tq=tq, tv=tl["tv"])

    k_hm = k_bf.reshape(DA_HEADS, batch, t_len, LANES)
    if past_len:
        n_pad = t_k - past_len - t_len
        cache_hm = lambda c: c.reshape(batch, past_len, DA_HEADS, LANES).transpose(2, 0, 1, 3).astype(BF16)
        k_hm = jnp.concatenate([cache_hm(past_k), k_hm, jnp.zeros((DA_HEADS, batch, n_pad, LANES), BF16)], axis=2)
        vt_bf = jnp.concatenate([jnp.swapaxes(cache_hm(past_v), 2, 3)[:, :, None], vt_bf,
                                 jnp.zeros((DA_HEADS, batch, 1, LANES, n_pad), BF16)], axis=4)

    att = _attention(prm["lam4"], qs_bf, k_hm, vt_bf, proj, prm["subln_col"], batch=batch, t_q=t_len, t_k=t_k,
                     tq=tq, tk=tk, q_pos0=past_len, kv_len=past_len + t_len, lam_init=lam_init)

    shift0_b = shift0.reshape(batch, SHIFT_BLOCKS, 1, LANES)
    rw, state_p, shift_b = _rwkv(proj, shift0_b, _pack_state(wkv0), prm, batch=batch, t_len=t_len,
                                 c_len=tl["c_len"])

    y = _out_proj(att, rw, prm["w_att"], prm["w_rw"], x2, tm=tl["tm_out"])
    return (y.reshape(batch, t_len, D_MODEL),
            k_out.reshape(batch, t_len, DA_HEADS, 2, DA_QK_DIM),
            v_out.reshape(batch, t_len, DA_HEADS, DA_V_DIM),
            _unpack_state(state_p),
            shift_b.reshape(batch, 1, RW_SHIFT_COLS))


def kernel(x_prompt, x_sample, cache_attn_k, cache_attn_v, state_rwkv_wkv, state_rwkv_shift, norm_g, w_in, q_norm_g, k_norm_g, lambda_q1, lambda_k1, lambda_q2, lambda_k2, subln_g, shift_mu, w0, w_up, a0, a_up, k_k, k_a, r_k, lnx_g, lnx_b, w_out):
    depth = w_in.shape[0]
    batch = x_prompt.shape[0]
    dt = x_prompt.dtype
    yp, ys = x_prompt, x_sample
    outs = [[] for _ in range(8)]
    for l in range(depth):
        prm = _prepare_params(norm_g[l], w_in[l], q_norm_g[l], k_norm_g[l], lambda_q1[l], lambda_k1[l], lambda_q2[l],
                              lambda_k2[l], subln_g[l], shift_mu[l], w0[l], w_up[l], a0[l], a_up[l], k_k[l], k_a[l],
                              r_k[l], lnx_g[l], lnx_b[l], w_out[l])
        yp, kp, vp, sp, hp = _layer(
            yp, jnp.zeros((batch, 0, DA_HEADS, 2, DA_QK_DIM), dt), jnp.zeros((batch, 0, DA_HEADS, DA_V_DIM), dt),
            jnp.zeros((batch, RW_HEADS, RW_HEAD, RW_HEAD), dt), jnp.zeros((batch, 1, RW_SHIFT_COLS), dt), l, prm)
        ys, ksn, vsn, ssn, hsn = _layer(ys, cache_attn_k[l], cache_attn_v[l], state_rwkv_wkv[l],
                                        state_rwkv_shift[l], l, prm)
        for lst, val in zip(outs, (kp, vp, sp, hp, ksn, vsn, ssn, hsn)):
            lst.append(val)
    return (yp, ys) + tuple(jnp.stack(o) for o in outs)
```

```python
import functools
import math

import jax
import jax.numpy as jnp
from jax import lax
from jax.experimental import pallas as pl
from jax.experimental.pallas import tpu as pltpu

F32 = jnp.float32
BF16 = jnp.bfloat16

D_MODEL = 2048
CHUNK = 64
DA_HEADS = 8
DA_V_DIM = 128
DA_QK_DIM = 64
DA_WIDTH = DA_HEADS * DA_V_DIM
RW_WIDTH = 1024
RW_HEAD = 64
RW_HEADS = RW_WIDTH // RW_HEAD
RW_LORA = 64
RW_SHIFT_COLS = 3 * RW_WIDTH + 2 * RW_LORA
IN_COLS = 4 * DA_WIDTH + RW_SHIFT_COLS + RW_WIDTH
ROPE_THETA = 10000.0
NORM_EPS = 1e-6
LNX_EPS = 64e-5
NEG_INF = -1e30
LOG2_E = math.log2(math.e)

LANES = 128
MXU_COLS = 256
N_COL_BLOCKS = -(-IN_COLS // MXU_COLS) * (MXU_COLS // LANES)
QB0, KB0, VB0, GAB0, RB0, KRB0, VRB0, GRB0, WAB0 = 0, 8, 16, 24, 32, 40, 48, 56, 64
RW_BLOCKS = RW_WIDTH // LANES
SHIFT_BLOCKS = RW_SHIFT_COLS // LANES

VMEM_LIMIT = 56 * 1024 * 1024

NT_DIMS = (((1,), (1,)), ((), ()))


def _cparams(sem):
    return pltpu.CompilerParams(dimension_semantics=sem, vmem_limit_bytes=VMEM_LIMIT)


def _bdot(a, b):
    return jnp.dot(a.astype(BF16), b.astype(BF16), preferred_element_type=F32)


def _split2(x):
    hi = x.astype(BF16)
    return hi, (x - hi.astype(F32)).astype(BF16)


def _group_sums(x, lo):
    s_lo = jnp.sum(jnp.where(lo, x, 0.0), axis=-1, keepdims=True)
    s_hi = jnp.sum(jnp.where(lo, 0.0, x), axis=-1, keepdims=True)
    return jnp.where(lo, s_lo, s_hi)


def _silu(g):
    return g * jax.nn.sigmoid(g)


def _in_proj_kernel(x_ref, g_ref, w_ref, o_ref, xn_ref):
    @pl.when(pl.program_id(1) == 0)
    def _():
        xf = x_ref[...]
        ms = jnp.mean(xf * xf, axis=-1, keepdims=True)
        xn_ref[...] = (xf * lax.rsqrt(ms + NORM_EPS) * g_ref[...]).astype(BF16)

    acc = jnp.dot(xn_ref[...], w_ref[...], preferred_element_type=F32)
    for c in range(o_ref.shape[0]):
        o_ref[c] = acc[:, c * LANES:(c + 1) * LANES]


def _in_proj(x2, norm_g, w_bf, *, tm, bn):
    n = x2.shape[0]
    return pl.pallas_call(
        _in_proj_kernel,
        out_shape=jax.ShapeDtypeStruct((N_COL_BLOCKS, n, LANES), F32),
        grid=(n // tm, N_COL_BLOCKS // bn),
        in_specs=[
            pl.BlockSpec((tm, D_MODEL), lambda i, j: (i, 0)),
            pl.BlockSpec((1, D_MODEL), lambda i, j: (0, 0)),
            pl.BlockSpec((D_MODEL, bn * LANES), lambda i, j: (0, j)),
        ],
        out_specs=pl.BlockSpec((bn, tm, LANES), lambda i, j: (j, i, 0)),
        scratch_shapes=[pltpu.VMEM((tm, D_MODEL), BF16)],
        compiler_params=_cparams(("parallel", "arbitrary")),
        name="in_proj",
    )(x2, norm_g, w_bf)


def _qkv_kernel(p_ref, gq_ref, gk_ref, invf_ref, kout_ref, vout_ref, qs_ref, kbf_ref, vt_ref, *, tq, tv, pos0):
    ti = pl.program_id(1)
    row = lax.broadcasted_iota(jnp.int32, (tq, LANES), 0)
    lane = lax.broadcasted_iota(jnp.int32, (tq, LANES), 1)
    ang = (pos0 + ti * tq + row).astype(F32) * invf_ref[...]
    cos = jnp.cos(ang)
    sin = jnp.sin(ang)
    first_half = jnp.bitwise_and(lane, DA_QK_DIM - 1) < DA_QK_DIM // 2
    sin_signed = jnp.where(first_half, -sin, sin)
    lo = lane < DA_QK_DIM

    def norm_rope(x, g):
        ms = _group_sums(x * x, lo) * (1.0 / DA_QK_DIM)
        xn = x * lax.rsqrt(ms + NORM_EPS) * g
        rot = jnp.where(first_half, pltpu.roll(xn, LANES - 32, 1), pltpu.roll(xn, 32, 1))
        return xn * cos + rot * sin_signed

    gq = gq_ref[...]
    gk = gk_ref[...]
    for h in range(DA_HEADS):
        q = norm_rope(p_ref[QB0 + h], gq) * (LOG2_E * DA_QK_DIM ** -0.5)
        qs_ref[h, 0:tq, :] = jnp.where(lo, q, 0.0).astype(BF16)
        qs_ref[h, tq:2 * tq, :] = jnp.where(lo, 0.0, q).astype(BF16)
        k = norm_rope(p_ref[KB0 + h], gk)
        kout_ref[:, h * LANES:(h + 1) * LANES] = k
        kbf_ref[h] = k.astype(BF16)
        v = p_ref[VB0 + h]
        vout_ref[:, h * LANES:(h + 1) * LANES] = v
        for s in range(tq // tv):
            vt_ref[h, s] = v[s * tv:(s + 1) * tv, :].T.astype(BF16)


def _qkv_prep(proj, gq, gk, invf, *, batch, t_len, pos0, tq, tv):
    n = batch * t_len
    nq = t_len // tq
    kern = functools.partial(_qkv_kernel, tq=tq, tv=tv, pos0=pos0)
    row_major = jax.ShapeDtypeStruct((n, DA_WIDTH), F32)
    rm_spec = pl.BlockSpec((tq, DA_WIDTH), lambda b, t: (b * nq + t, 0))
    vec_spec = pl.BlockSpec((1, LANES), lambda b, t: (0, 0))
    return pl.pallas_call(
        kern,
        out_shape=(row_major, row_major,
                   jax.ShapeDtypeStruct((DA_HEADS, 2 * n, LANES), BF16),
                   jax.ShapeDtypeStruct((DA_HEADS, n, LANES), BF16),
                   jax.ShapeDtypeStruct((DA_HEADS, batch, t_len // tv, LANES, tv), BF16)),
        grid=(batch, nq),
        in_specs=[pl.BlockSpec((3 * DA_HEADS, tq, LANES), lambda b, t: (0, b * nq + t, 0)),
                  vec_spec, vec_spec, vec_spec],
        out_specs=(rm_spec, rm_spec,
                   pl.BlockSpec((DA_HEADS, 2 * tq, LANES), lambda b, t: (0, b * nq + t, 0)),
                   pl.BlockSpec((DA_HEADS, tq, LANES), lambda b, t: (0, b * nq + t, 0)),
                   pl.BlockSpec((DA_HEADS, None, tq // tv, LANES, tv), lambda b, t: (0, b, t, 0, 0))),
        compiler_params=_cparams(("parallel", "parallel")),
        name="qkv_prep",
    )(proj, gq, gk, invf)


def _attn_kernel(lam_ref, qs_ref, k_ref, vt_ref, g_ref, sg_ref, o_ref, m_ref, l_ref, acc_ref, sta_ref, stb_ref,
                 *, tq, tk, n_tail, q_pos0, kv_len, lam_init):
    qi = pl.program_id(2)
    qs = qs_ref[...]

    m_ref[...] = jnp.full(m_ref.shape, NEG_INF, F32)
    l_ref[...] = jnp.zeros(l_ref.shape, F32)
    acc_ref[...] = jnp.zeros(acc_ref.shape, F32)

    def scores_into(st_ref, j):
        k = k_ref[pl.ds(pl.multiple_of(j * tk, tk), tk), :]
        st_ref[...] = lax.dot_general(k, qs, NT_DIMS, preferred_element_type=F32)

    def consume(st_ref, j, masked):
        st = st_ref[...]
        if masked:
            k_pos = j * tk + lax.broadcasted_iota(jnp.int32, (tk, 2 * tq), 0)
            col = lax.broadcasted_iota(jnp.int32, (tk, 2 * tq), 1)
            q_pos = q_pos0 + qi * tq + jnp.bitwise_and(col, tq - 1)
            visible = (lax.shift_right_logical(k_pos, 6) <= lax.shift_right_logical(q_pos, 6)) & (k_pos < kv_len)
            st = jnp.where(visible, st, NEG_INF)
        m_prev = m_ref[...]
        m_new = jnp.maximum(m_prev, jnp.max(st, axis=0, keepdims=True))
        alpha = jnp.exp2(m_prev - m_new)
        p = jnp.exp2(st - m_new)
        l_ref[...] = alpha * l_ref[...] + jnp.sum(p, axis=0, keepdims=True)
        pv = jnp.dot(vt_ref[j], p.astype(BF16), preferred_element_type=F32)
        acc_ref[...] = alpha * acc_ref[...] + pv
        m_ref[...] = m_new

    n_full = (q_pos0 + qi * tq) // tk
    scores_into(sta_ref, 0)
    bufs = (sta_ref, stb_ref)

    def run_groups(first, n_groups, group):
        def body(i, carry):
            base = first + group * i
            for u in range(group):
                scores_into(bufs[(u + 1) % 2], base + u + 1)
                consume(bufs[u % 2], base + u, False)
            return carry
        lax.fori_loop(0, n_groups, body, 0)

    n_quads = n_full // 4
    run_groups(0, n_quads, 4)
    run_groups(4 * n_quads, (n_full // 2) % 2, 2)
    for t in range(n_tail):
        if t + 1 < n_tail:
            scores_into(bufs[(t + 1) % 2], n_full + t + 1)
        consume(bufs[t % 2], n_full + t, True)

    lam4 = lam_ref[...]
    lam = (jnp.exp(jnp.sum(lam4[0:1] * lam4[1:2], axis=-1, keepdims=True))
           - jnp.exp(jnp.sum(lam4[2:3] * lam4[3:4], axis=-1, keepdims=True)) + lam_init)
    on = acc_ref[...] / l_ref[...]
    ot = on[:, 0:tq] - lam * on[:, tq:2 * tq]
    ms = jnp.mean(ot * ot, axis=0, keepdims=True)
    ot = ot * lax.rsqrt(ms + NORM_EPS) * sg_ref[...] * (1.0 - lam_init)
    o_ref[...] = (ot.T * _silu(g_ref[...])).astype(o_ref.dtype)


def _attention(lam4, qs_bf, k_bf, vt_bf, proj, subln_col, *, batch, t_q, t_k, tq, tk, q_pos0, kv_len, lam_init):
    nq = t_q // tq
    nkb = t_k // tk
    n_tail = max(1, tq // tk)
    assert t_k % tk == 0 and q_pos0 + t_q <= t_k and tq & (tq - 1) == 0
    assert (tq == 2 * tk and q_pos0 == 0) or (nq == 1 and nkb == 1)
    kern = functools.partial(_attn_kernel, tq=tq, tk=tk, n_tail=n_tail, q_pos0=q_pos0, kv_len=kv_len,
                             lam_init=lam_init)
    return pl.pallas_call(
        kern,
        out_shape=jax.ShapeDtypeStruct((batch * t_q, DA_WIDTH), BF16),
        grid=(batch, DA_HEADS, nq),
        in_specs=[
            pl.BlockSpec((4, DA_QK_DIM), lambda b, h, i: (0, 0)),
            pl.BlockSpec((None, 2 * tq, LANES), lambda b, h, i: (h, b * nq + i, 0)),
            pl.BlockSpec((None, None, t_k, LANES), lambda b, h, i: (h, b, 0, 0)),
            pl.BlockSpec((None, None, nkb, LANES, tk), lambda b, h, i: (h, b, 0, 0, 0)),
            pl.BlockSpec((None, tq, LANES), lambda b, h, i: (GAB0 + h, b * nq + i, 0)),
            pl.BlockSpec((LANES, 1), lambda b, h, i: (0, 0)),
        ],
        out_specs=pl.BlockSpec((tq, LANES), lambda b, h, i: (b * nq + i, h)),
        scratch_shapes=[pltpu.VMEM((1, 2 * tq), F32), pltpu.VMEM((1, 2 * tq), F32),
                        pltpu.VMEM((LANES, 2 * tq), F32),
                        pltpu.VMEM((tk, 2 * tq), F32), pltpu.VMEM((tk, 2 * tq), F32)],
        compiler_params=_cparams(("parallel", "parallel", "arbitrary")),
        name="diff_attention",
    )(lam4, qs_bf, k_bf, vt_bf, proj, subln_col)


def _rwkv_kernel(r_ref, k_ref, v_ref, g_ref, wa_ref, shift0_ref, wkv0_ref, mu_ref, w0_ref, a0_ref, kk_ref,
                 ka_ref, rk_ref, lng_ref, lnb_ref, wup_ref, aup_ref,
                 out_ref, state_ref, shift_ref,
                 h_ref, prev_ref, lhs_s, rhs_s, vm_s, ends_s, pc_s, bonus_s, *, c_len):
    c = pl.program_id(1)
    last = pl.num_programs(1) - 1
    cl = c_len
    heads = range(RW_HEADS)

    @pl.when(c == 0)
    def _():
        h_ref[...] = wkv0_ref[0]
        prev_ref[...] = shift0_ref[0]
        for s in (lhs_s, rhs_s, vm_s, ends_s, pc_s, bonus_s):
            s[...] = jnp.zeros(s.shape, s.dtype)

    row = lax.broadcasted_iota(jnp.int32, (cl, LANES), 0)
    lane = lax.broadcasted_iota(jnp.int32, (cl, LANES), 1)
    lo = lane < RW_HEAD
    own = (lo, jnp.logical_not(lo))
    sr = jnp.bitwise_and(lax.broadcasted_iota(jnp.int32, (2 * cl, 2 * cl), 0), cl - 1)
    sc = jnp.bitwise_and(lax.broadcasted_iota(jnp.int32, (2 * cl, 2 * cl), 1), cl - 1)
    top = lax.broadcasted_iota(jnp.int32, (2 * cl, 2 * cl), 0) < cl
    score_mask = sc < sr + jnp.where(top, 0, 1)
    tr = lax.broadcasted_iota(jnp.int32, (cl, cl), 0)
    tc = lax.broadcasted_iota(jnp.int32, (cl, cl), 1)
    tril_incl = (tc <= tr).astype(BF16)
    hr = lax.broadcasted_iota(jnp.int32, (RW_HEAD, LANES), 0)
    hl = lax.broadcasted_iota(jnp.int32, (RW_HEAD, LANES), 1)
    diag = (hl == hr, hl == hr + RW_HEAD)
    h_own = (hl < RW_HEAD, hl >= RW_HEAD)
    has_prev = (jnp.zeros((RW_HEAD, LANES), jnp.int32) + c) >= 1
    zeros_cb = jnp.zeros((cl, LANES), BF16)
    zeros_h = jnp.zeros((RW_HEAD, LANES), BF16)
    n_double = int(math.log2(cl))
    mm = functools.partial(jnp.dot, preferred_element_type=F32)

    lhs_prev = [lhs_s[hd] for hd in heads]
    vm_prev = [vm_s[hd] for hd in heads]
    rhs_prev = [rhs_s[p] for p in range(RW_BLOCKS)]
    ends_prev = [ends_s[p] for p in range(RW_BLOCKS)]
    pc_prev = [pc_s[p] for p in range(RW_BLOCKS)]
    bonus_prev = [bonus_s[p] for p in range(RW_BLOCKS)]

    def token_shift(x, blk):
        prev = jnp.where(row == 0, prev_ref[blk], pltpu.roll(x, 1, 0))
        return x + mu_ref[blk] * (prev - x)

    wa_raw = wa_ref[0]
    xs_wa = token_shift(wa_raw, 3 * RW_BLOCKS)
    w_lora = _bdot(jnp.tanh(xs_wa), wup_ref[...])
    a_lora = _bdot(xs_wa, aup_ref[...])
    xr, xk, xv, log_decay = [], [], [], []
    for p in range(RW_BLOCKS):
        r_raw, k_raw, v_raw = r_ref[p], k_ref[p], v_ref[p]
        xr.append(token_shift(r_raw, p))
        xk.append(token_shift(k_raw, RW_BLOCKS + p))
        xv.append(token_shift(v_raw, 2 * RW_BLOCKS + p))
        prev_ref[p] = r_raw[cl - 1:cl, :]
        prev_ref[RW_BLOCKS + p] = k_raw[cl - 1:cl, :]
        prev_ref[2 * RW_BLOCKS + p] = v_raw[cl - 1:cl, :]
        neg = -(w0_ref[p] + w_lora[:, p * LANES:(p + 1) * LANES])
        softplus = jnp.maximum(neg, 0.0) + jnp.log(1.0 + jnp.exp(-jnp.abs(neg)))
        log_decay.append(-jnp.exp(-softplus - 0.5))
    prev_ref[3 * RW_BLOCKS] = wa_raw[cl - 1:cl, :]
    ld_all = jnp.concatenate(log_decay, axis=1)
    ld_hi = ld_all.astype(BF16)
    ld_mid, ld_lo = _split2(ld_all - ld_hi.astype(F32))

    scores = [jnp.where(score_mask, lax.dot_general(lhs_prev[hd], rhs_prev[hd // 2], NT_DIMS,
                                                    preferred_element_type=F32), 0.0) for hd in heads]
    x = [lhs_prev[hd][0:cl, :].astype(F32)
         + mm(scores[hd][0:cl, :].astype(BF16), jnp.concatenate([zeros_cb, vm_prev[hd]], axis=0)) for hd in heads]
    cs_all = mm(tril_incl, ld_hi) + mm(tril_incl, ld_mid) + mm(tril_incl, ld_lo)
    a_pow = [scores[hd][0:cl, 0:cl] for hd in heads]
    for step in range(n_double):
        for hd in heads:
            if step + 1 < n_double:
                res = _bdot(a_pow[hd], jnp.concatenate([x[hd], a_pow[hd]], axis=1))
                x[hd] = x[hd] + res[:, 0:LANES]
                a_pow[hd] = res[:, LANES:LANES + cl]
            else:
                x[hd] = x[hd] + _bdot(a_pow[hd], x[hd])
    fin = []
    for hd in heads:
        p, hh = divmod(hd, 2)
        lhs_fin = jnp.concatenate([ends_prev[p][hh * RW_HEAD:(hh + 1) * RW_HEAD, :],
                                   scores[hd][cl:2 * cl, :].astype(BF16)], axis=0)
        fin.append(mm(lhs_fin, jnp.concatenate([x[hd].astype(BF16), vm_prev[hd]], axis=0)))
    y_m = []
    for hd in heads:
        p, hh = divmod(hd, 2)
        m_packed = fin[hd][0:RW_HEAD, :] + jnp.where(diag[hh], pc_prev[p], 0.0)
        g_packed = fin[hd][RW_HEAD:RW_HEAD + cl, :] + lhs_prev[hd][cl:2 * cl, :].astype(F32)
        h_old = h_ref[hd]
        h_hi, h_lo = _split2(h_old)
        pad = (lambda t: jnp.concatenate([t, zeros_h], axis=0)) if hh == 0 else (
            lambda t: jnp.concatenate([zeros_h, t], axis=0))
        h_hi, h_lo = pad(h_hi), pad(h_lo)
        y_m.append(mm(g_packed.astype(BF16), h_hi) + jnp.where(own[1 - hh], g_packed, 0.0))
        m_hi, m_lo = _split2(m_packed)
        h_new = mm(m_hi, h_hi) + mm(m_hi, h_lo) + mm(m_lo, h_hi) + jnp.where(h_own[1 - hh], m_packed, 0.0)
        h_ref[hd] = jnp.where(has_prev, h_new, h_old)

    inv_n = 1.0 / RW_HEAD
    for p in range(RW_BLOCKS):
        y = pltpu.roll(y_m[2 * p] + y_m[2 * p + 1], RW_HEAD, 1)
        yc = y - _group_sums(y, lo) * inv_n
        var = _group_sums(yc * yc, lo) * inv_n
        yn = yc * lax.rsqrt(var + LNX_EPS) * lng_ref[p] + lnb_ref[p]
        out_ref[:, p * LANES:(p + 1) * LANES] = ((yn + bonus_prev[p]) * _silu(g_ref[p])).astype(out_ref.dtype)

    for p in range(RW_BLOCKS):
        cols = slice(p * LANES, (p + 1) * LANES)
        a_sig = jax.nn.sigmoid(a0_ref[p] + a_lora[:, cols])
        kk_raw = xk[p] * kk_ref[p]
        kk = kk_raw * lax.rsqrt(jnp.maximum(_group_sums(kk_raw * kk_raw, lo), 1e-24))
        k_mod = xk[p] * (1.0 + (a_sig - 1.0) * ka_ref[p])
        b_vec = kk * a_sig
        cs = cs_all[:, cols]
        cs_last = cs[cl - 1:cl, :]
        decay_out = jnp.exp(-cs)
        decay_to_end = jnp.exp(cs_last - cs)
        a_t = -kk * jnp.exp(cs - log_decay[p])
        r_t = xr[p] * jnp.exp(cs)
        rhs_s[p] = jnp.concatenate([b_vec * decay_out, k_mod * decay_out], axis=0).astype(BF16)
        ends_s[p] = jnp.concatenate([b_vec * decay_to_end, k_mod * decay_to_end], axis=0).T.astype(BF16)
        pc_s[p] = jnp.exp(cs_last)
        bonus_s[p] = _group_sums(xr[p] * k_mod * rk_ref[p], lo) * xv[p]
        v_roll = pltpu.roll(xv[p], RW_HEAD, 1)
        for hh in range(2):
            hd = 2 * p + hh
            lhs_s[hd] = jnp.concatenate([jnp.where(own[hh], a_t, 0.0), jnp.where(own[hh], r_t, 0.0)],
                                        axis=0).astype(BF16)
            vm_s[hd] = jnp.where(own[1 - hh], v_roll, 0.0).astype(BF16)

    @pl.when(c == last)
    def _():
        state_ref[0] = h_ref[...]
        shift_ref[0] = prev_ref[...]


def _rwkv(proj, shift0_b, wkv0_p, prm, *, batch, t_len, c_len):
    nc = t_len // c_len
    n = batch * t_len
    kern = functools.partial(_rwkv_kernel, c_len=c_len)
    cur = lambda b, c: b * nc + jnp.minimum(c, nc - 1)
    prv = lambda b, c: b * nc + jnp.maximum(c - 1, 0)
    blk8 = lambda base, chunk: pl.BlockSpec((RW_BLOCKS, c_len, LANES),
                                            lambda b, c: (base // RW_BLOCKS, chunk(b, c), 0))
    vec8 = pl.BlockSpec((RW_BLOCKS, 1, LANES), lambda b, c: (0, 0, 0))
    lora = pl.BlockSpec((LANES, RW_WIDTH), lambda b, c: (0, 0))
    state_spec = pl.BlockSpec((1, RW_HEADS, RW_HEAD, LANES), lambda b, c: (b, 0, 0, 0))
    shift_spec = pl.BlockSpec((1, SHIFT_BLOCKS, 1, LANES), lambda b, c: (b, 0, 0, 0))
    return pl.pallas_call(
        kern,
        out_shape=(
            jax.ShapeDtypeStruct((n, RW_WIDTH), BF16),
            jax.ShapeDtypeStruct((batch, RW_HEADS, RW_HEAD, LANES), F32),
            jax.ShapeDtypeStruct((batch, SHIFT_BLOCKS, 1, LANES), F32),
        ),
        grid=(batch, nc + 1),
        in_specs=[
            blk8(RB0, cur), blk8(KRB0, cur), blk8(VRB0, cur), blk8(GRB0, prv),
            pl.BlockSpec((1, c_len, LANES), lambda b, c: (WAB0, cur(b, c), 0)),
            shift_spec, state_spec,
            pl.BlockSpec((SHIFT_BLOCKS, 1, LANES), lambda b, c: (0, 0, 0)),
            vec8, vec8, vec8, vec8, vec8, vec8, vec8, lora, lora,
        ],
        out_specs=(pl.BlockSpec((c_len, RW_WIDTH), lambda b, c: (prv(b, c), 0)), state_spec, shift_spec),
        scratch_shapes=[
            pltpu.VMEM((RW_HEADS, RW_HEAD, LANES), F32),
            pltpu.VMEM((SHIFT_BLOCKS, 1, LANES), F32),
            pltpu.VMEM((RW_HEADS, 2 * c_len, LANES), BF16),
            pltpu.VMEM((RW_BLOCKS, 2 * c_len, LANES), BF16),
            pltpu.VMEM((RW_HEADS, c_len, LANES), BF16),
            pltpu.VMEM((RW_BLOCKS, LANES, 2 * c_len), BF16),
            pltpu.VMEM((RW_BLOCKS, 1, LANES), F32),
            pltpu.VMEM((RW_BLOCKS, c_len, LANES), F32),
        ],
        compiler_params=_cparams(("parallel", "arbitrary")),
        name="rwkv7_chunked",
    )(proj, proj, proj, proj, proj, shift0_b, wkv0_p, prm["mu"], prm["w0"], prm["a0"], prm["k_k"], prm["k_a"],
      prm["r_k"], prm["lnx_g"], prm["lnx_b"], prm["w_up"], prm["a_up"])


def _pack_state(wkv):
    h = jnp.swapaxes(wkv, -1, -2)
    z = jnp.zeros_like(h)
    even = jnp.concatenate([z, h], axis=-1)
    odd = jnp.concatenate([h, z], axis=-1)
    is_even = (jnp.arange(RW_HEADS) % 2 == 0)[None, :, None, None]
    return jnp.where(is_even, even, odd)


def _unpack_state(hp):
    is_even = (jnp.arange(RW_HEADS) % 2 == 0)[None, :, None, None]
    h = jnp.where(is_even, hp[..., RW_HEAD:], hp[..., :RW_HEAD])
    return jnp.swapaxes(h, -1, -2)


def _out_proj_kernel(att_ref, rw_ref, wa_ref, wr_ref, x_ref, o_ref):
    acc = jnp.dot(att_ref[...], wa_ref[...], preferred_element_type=F32)
    acc = acc + jnp.dot(rw_ref[...], wr_ref[...], preferred_element_type=F32)
    o_ref[...] = x_ref[...] + acc


def _out_proj(att, rw, w_att, w_rw, x2, *, tm):
    n = x2.shape[0]
    return pl.pallas_call(
        _out_proj_kernel,
        out_shape=jax.ShapeDtypeStruct((n, D_MODEL), F32),
        grid=(n // tm,),
        in_specs=[
            pl.BlockSpec((tm, DA_WIDTH), lambda i: (i, 0)),
            pl.BlockSpec((tm, RW_WIDTH), lambda i: (i, 0)),
            pl.BlockSpec((DA_WIDTH, D_MODEL), lambda i: (0, 0)),
            pl.BlockSpec((RW_WIDTH, D_MODEL), lambda i: (0, 0)),
            pl.BlockSpec((tm, D_MODEL), lambda i: (i, 0)),
        ],
        out_specs=pl.BlockSpec((tm, D_MODEL), lambda i: (i, 0)),
        compiler_params=_cparams(("parallel",)),
        name="out_proj",
    )(att, rw, w_att, w_rw, x2)


def _blocks(v):
    return v.reshape(-1, 1, LANES)


def _prepare_params(norm_g, w_in, q_norm_g, k_norm_g, lq1, lk1, lq2, lk2, subln_g, shift_mu, w0, w_up, a0, a_up,
                    k_k, k_a, r_k, lnx_g, lnx_b, w_out):
    rw0 = 4 * DA_WIDTH
    lora0 = rw0 + 3 * RW_WIDTH
    g0 = rw0 + RW_SHIFT_COLS
    w_perm = jnp.concatenate([w_in[:, :lora0], w_in[:, g0:], w_in[:, lora0:g0],
                              jnp.zeros((D_MODEL, N_COL_BLOCKS * LANES - IN_COLS), w_in.dtype)], axis=1)
    half = DA_QK_DIM // 2
    inv_freq = ROPE_THETA ** (-jnp.arange(half, dtype=F32) / half)
    zeros = jnp.zeros((RW_LORA, RW_WIDTH), F32)
    return dict(
        norm_g=norm_g.reshape(1, D_MODEL),
        w_in=w_perm.astype(BF16),
        gq=jnp.tile(q_norm_g, 2).reshape(1, LANES),
        gk=jnp.tile(k_norm_g, 2).reshape(1, LANES),
        invf=jnp.tile(inv_freq, 4).reshape(1, LANES),
        lam4=jnp.stack([lq1, lk1, lq2, lk2]),
        subln_col=subln_g.reshape(LANES, 1),
        mu=_blocks(shift_mu),
        w0=_blocks(w0), a0=_blocks(a0), k_k=_blocks(k_k), k_a=_blocks(k_a), r_k=_blocks(r_k.reshape(-1)),
        lnx_g=_blocks(lnx_g), lnx_b=_blocks(lnx_b),
        w_up=jnp.concatenate([w_up, zeros], axis=0).astype(BF16),
        a_up=jnp.concatenate([zeros, a_up], axis=0).astype(BF16),
        w_att=w_out[:DA_WIDTH].astype(BF16),
        w_rw=w_out[DA_WIDTH:].astype(BF16),
    )


def _tiles(batch, t_len, past_len):
    n = batch * t_len
    if past_len == 0:
        tq = min(t_len, 512)
        tk = tv = tq // 2
        t_k = t_len
    else:
        tq = tv = t_len
        t_k = tk = -(-(past_len + t_len) // LANES) * LANES
    return dict(tm=min(n, 1024), bn=6, tq=tq, tk=tk, tv=tv, t_k=t_k, c_len=min(t_len, CHUNK), tm_out=min(n, 512))


def _layer(x, past_k, past_v, wkv0, shift0, layer_idx, prm):
    batch, t_len, _ = x.shape
    past_len = past_k.shape[1]
    n = batch * t_len
    assert t_len & (t_len - 1) == 0
    tl = _tiles(batch, t_len, past_len)
    tq, tk, t_k = tl["tq"], tl["tk"], tl["t_k"]
    lam_init = 0.8 - 0.6 * math.exp(-0.3 * layer_idx)

    x2 = x.reshape(n, D_MODEL)
    proj = _in_proj(x2, prm["norm_g"], prm["w_in"], tm=tl["tm"], bn=tl["bn"])
    k_out, v_out, qs_bf, k_bf, vt_bf = _qkv_prep(proj, prm["gq"], prm["gk"], prm["invf"], batch=batch, t_len=t_len,
                                                 pos0=past_len, tq=tq, tv=tl["tv"])

    k_hm = k_bf.reshape(DA_HEADS, batch, t_len, LANES)
    if past_len:
        n_pad = t_k - past_len - t_len
        cache_hm = lambda c: c.reshape(batch, past_len, DA_HEADS, LANES).transpose(2, 0, 1, 3).astype(BF16)
        k_hm = jnp.concatenate([cache_hm(past_k), k_hm, jnp.zeros((DA_HEADS, batch, n_pad, LANES), BF16)], axis=2)
        vt_bf = jnp.concatenate([jnp.swapaxes(cache_hm(past_v), 2, 3)[:, :, None], vt_bf,
                                 jnp.zeros((DA_HEADS, batch, 1, LANES, n_pad), BF16)], axis=4)

    att = _attention(prm["lam4"], qs_bf, k_hm, vt_bf, proj, prm["subln_col"], batch=batch, t_q=t_len, t_k=t_k,
                     tq=tq, tk=tk, q_pos0=past_len, kv_len=past_len + t_len, lam_init=lam_init)

    shift0_b = shift0.reshape(batch, SHIFT_BLOCKS, 1, LANES)
    rw, state_p, shift_b = _rwkv(proj, shift0_b, _pack_state(wkv0), prm, batch=batch, t_len=t_len,
                                 c_len=tl["c_len"])

    y = _out_proj(att, rw, prm["w_att"], prm["w_rw"], x2, tm=tl["tm_out"])
    return (y.reshape(batch, t_len, D_MODEL),
            k_out.reshape(batch, t_len, DA_HEADS, 2, DA_QK_DIM),
            v_out.reshape(batch, t_len, DA_HEADS, DA_V_DIM),
            _unpack_state(state_p),
            shift_b.reshape(batch, 1, RW_SHIFT_COLS))


def kernel(x_prompt, x_sample, cache_attn_k, cache_attn_v, state_rwkv_wkv, state_rwkv_shift, norm_g, w_in, q_norm_g, k_norm_g, lambda_q1, lambda_k1, lambda_q2, lambda_k2, subln_g, shift_mu, w0, w_up, a0, a_up, k_k, k_a, r_k, lnx_g, lnx_b, w_out):
    depth = w_in.shape[0]
    batch = x_prompt.shape[0]
    dt = x_prompt.dtype
    yp, ys = x_prompt, x_sample
    outs = [[] for _ in range(8)]
    for l in range(depth):
        prm = _prepare_params(norm_g[l], w_in[l], q_norm_g[l], k_norm_g[l], lambda_q1[l], lambda_k1[l], lambda_q2[l],
                              lambda_k2[l], subln_g[l], shift_mu[l], w0[l], w_up[l], a0[l], a_up[l], k_k[l], k_a[l],
                              r_k[l], lnx_g[l], lnx_b[l], w_out[l])
        yp, kp, vp, sp, hp = _layer(
            yp, jnp.zeros((batch, 0, DA_HEADS, 2, DA_QK_DIM), dt), jnp.zeros((batch, 0, DA_HEADS, DA_V_DIM), dt),
            jnp.zeros((batch, RW_HEADS, RW_HEAD, RW_HEAD), dt), jnp.zeros((batch, 1, RW_SHIFT_COLS), dt), l, prm)
        ys, ksn, vsn, ssn, hsn = _layer(ys, cache_attn_k[l], cache_attn_v[l], state_rwkv_wkv[l],
                                        state_rwkv_shift[l], l, prm)
        for lst, val in zip(outs, (kp, vp, sp, hp, ksn, vsn, ssn, hsn)):
            lst.append(val)
    return (yp, ys) + tuple(jnp.stack(o) for o in outs)
```

```python
import functools
import math

import jax
import jax.numpy as jnp
from jax import lax
from jax.experimental import pallas as pl
from jax.experimental.pallas import tpu as pltpu

F32 = jnp.float32
BF16 = jnp.bfloat16

D_MODEL = 2048
CHUNK = 64
DA_HEADS = 8
DA_V_DIM = 128
DA_QK_DIM = 64
DA_WIDTH = DA_HEADS * DA_V_DIM
RW_WIDTH = 1024
RW_HEAD = 64
RW_HEADS = RW_WIDTH // RW_HEAD
RW_LORA = 64
RW_SHIFT_COLS = 3 * RW_WIDTH + 2 * RW_LORA
IN_COLS = 4 * DA_WIDTH + RW_SHIFT_COLS + RW_WIDTH
ROPE_THETA = 10000.0
NORM_EPS = 1e-6
LNX_EPS = 64e-5
NEG_INF = -1e30
LOG2_E = math.log2(math.e)
MAX_FIXED_OFFSET = 60.0

LANES = 128
MXU_COLS = 256
N_COL_BLOCKS = -(-IN_COLS // MXU_COLS) * (MXU_COLS // LANES)
QB0, KB0, VB0, GAB0, RB0, KRB0, VRB0, GRB0, WAB0 = 0, 8, 16, 24, 32, 40, 48, 56, 64
RW_BLOCKS = RW_WIDTH // LANES
SHIFT_BLOCKS = RW_SHIFT_COLS // LANES

VMEM_LIMIT = 56 * 1024 * 1024

NT_DIMS = (((1,), (1,)), ((), ()))


def _cparams(sem):
    return pltpu.CompilerParams(dimension_semantics=sem, vmem_limit_bytes=VMEM_LIMIT)


def _bdot(a, b):
    return jnp.dot(a.astype(BF16), b.astype(BF16), preferred_element_type=F32)


def _split2(x):
    hi = x.astype(BF16)
    return hi, (x - hi.astype(F32)).astype(BF16)


def _group_sums(x, lo):
    s_lo = jnp.sum(jnp.where(lo, x, 0.0), axis=-1, keepdims=True)
    s_hi = jnp.sum(jnp.where(lo, 0.0, x), axis=-1, keepdims=True)
    return jnp.where(lo, s_lo, s_hi)


def _silu(g):
    return g * jax.nn.sigmoid(g)


def _in_proj_kernel(x_ref, g_ref, w_ref, o_ref, xn_ref):
    @pl.when(pl.program_id(1) == 0)
    def _():
        xf = x_ref[...]
        ms = jnp.mean(xf * xf, axis=-1, keepdims=True)
        xn_ref[...] = (xf * lax.rsqrt(ms + NORM_EPS) * g_ref[...]).astype(BF16)

    acc = jnp.dot(xn_ref[...], w_ref[...], preferred_element_type=F32)
    for c in range(o_ref.shape[0]):
        o_ref[c] = acc[:, c * LANES:(c + 1) * LANES]


def _in_proj(x2, norm_g, w_bf, *, tm, bn):
    n = x2.shape[0]
    return pl.pallas_call(
        _in_proj_kernel,
        out_shape=jax.ShapeDtypeStruct((N_COL_BLOCKS, n, LANES), F32),
        grid=(n // tm, N_COL_BLOCKS // bn),
        in_specs=[
            pl.BlockSpec((tm, D_MODEL), lambda i, j: (i, 0)),
            pl.BlockSpec((1, D_MODEL), lambda i, j: (0, 0)),
            pl.BlockSpec((D_MODEL, bn * LANES), lambda i, j: (0, j)),
        ],
        out_specs=pl.BlockSpec((bn, tm, LANES), lambda i, j: (j, i, 0)),
        scratch_shapes=[pltpu.VMEM((tm, D_MODEL), BF16)],
        compiler_params=_cparams(("parallel", "arbitrary")),
        name="in_proj",
    )(x2, norm_g, w_bf)


def _qkv_kernel(p_ref, gq_ref, gk_ref, invf_ref, kout_ref, vout_ref, qs_ref, kbf_ref, vt_ref, *, tq, tv, pos0):
    ti = pl.program_id(1)
    row = lax.broadcasted_iota(jnp.int32, (tq, LANES), 0)
    lane = lax.broadcasted_iota(jnp.int32, (tq, LANES), 1)
    ang = (pos0 + ti * tq + row).astype(F32) * invf_ref[...]
    cos = jnp.cos(ang)
    sin = jnp.sin(ang)
    first_half = jnp.bitwise_and(lane, DA_QK_DIM - 1) < DA_QK_DIM // 2
    sin_signed = jnp.where(first_half, -sin, sin)
    lo = lane < DA_QK_DIM

    def norm_rope(x, g):
        ms = _group_sums(x * x, lo) * (1.0 / DA_QK_DIM)
        xn = x * lax.rsqrt(ms + NORM_EPS) * g
        rot = jnp.where(first_half, pltpu.roll(xn, LANES - 32, 1), pltpu.roll(xn, 32, 1))
        return xn * cos + rot * sin_signed

    gq = gq_ref[...]
    gk = gk_ref[...]
    for h in range(DA_HEADS):
        q = norm_rope(p_ref[QB0 + h], gq) * (LOG2_E * DA_QK_DIM ** -0.5)
        qs_ref[h, 0:tq, :] = jnp.where(lo, q, 0.0).astype(BF16)
        qs_ref[h, tq:2 * tq, :] = jnp.where(lo, 0.0, q).astype(BF16)
        k = norm_rope(p_ref[KB0 + h], gk)
        kout_ref[:, h * LANES:(h + 1) * LANES] = k
        kbf_ref[h] = k.astype(BF16)
        v = p_ref[VB0 + h]
        vout_ref[:, h * LANES:(h + 1) * LANES] = v
        for s in range(tq // tv):
            vt_ref[h, s] = v[s * tv:(s + 1) * tv, :].T.astype(BF16)


def _qkv_prep(proj, gq, gk, invf, *, batch, t_len, pos0, tq, tv):
    n = batch * t_len
    nq = t_len // tq
    kern = functools.partial(_qkv_kernel, tq=tq, tv=tv, pos0=pos0)
    row_major = jax.ShapeDtypeStruct((n, DA_WIDTH), F32)
    rm_spec = pl.BlockSpec((tq, DA_WIDTH), lambda b, t: (b * nq + t, 0))
    vec_spec = pl.BlockSpec((1, LANES), lambda b, t: (0, 0))
    return pl.pallas_call(
        kern,
        out_shape=(row_major, row_major,
                   jax.ShapeDtypeStruct((DA_HEADS, 2 * n, LANES), BF16),
                   jax.ShapeDtypeStruct((DA_HEADS, n, LANES), BF16),
                   jax.ShapeDtypeStruct((DA_HEADS, batch, t_len // tv, LANES, tv), BF16)),
        grid=(batch, nq),
        in_specs=[pl.BlockSpec((3 * DA_HEADS, tq, LANES), lambda b, t: (0, b * nq + t, 0)),
                  vec_spec, vec_spec, vec_spec],
        out_specs=(rm_spec, rm_spec,
                   pl.BlockSpec((DA_HEADS, 2 * tq, LANES), lambda b, t: (0, b * nq + t, 0)),
                   pl.BlockSpec((DA_HEADS, tq, LANES), lambda b, t: (0, b * nq + t, 0)),
                   pl.BlockSpec((DA_HEADS, None, tq // tv, LANES, tv), lambda b, t: (0, b, t, 0, 0))),
        compiler_params=_cparams(("parallel", "parallel")),
        name="qkv_prep",
    )(proj, gq, gk, invf)


def _attn_finish(on, lam4, sg_col, g, tq, lam_init):
    lam = (jnp.exp(jnp.sum(lam4[0:1] * lam4[1:2], axis=-1, keepdims=True))
           - jnp.exp(jnp.sum(lam4[2:3] * lam4[3:4], axis=-1, keepdims=True)) + lam_init)
    ot = on[:, 0:tq] - lam * on[:, tq:2 * tq]
    ms = jnp.mean(ot * ot, axis=0, keepdims=True)
    ot = ot * lax.rsqrt(ms + NORM_EPS) * sg_col * (1.0 - lam_init)
    return ot.T * _silu(g)


def _attn_kernel(lam_ref, qs_ref, k_ref, vt_ref, g_ref, sg_ref, o_ref, m_ref, l_ref, acc_ref, sta_ref, stb_ref,
                 *, tq, tk, n_tail, q_pos0, kv_len, lam_init):
    qi = pl.program_id(2)
    qs = qs_ref[...]

    m_ref[...] = jnp.full(m_ref.shape, NEG_INF, F32)
    l_ref[...] = jnp.zeros(l_ref.shape, F32)
    acc_ref[...] = jnp.zeros(acc_ref.shape, F32)

    def scores_into(st_ref, j):
        k = k_ref[pl.ds(pl.multiple_of(j * tk, tk), tk), :]
        st_ref[...] = lax.dot_general(k, qs, NT_DIMS, preferred_element_type=F32)

    def consume(st_ref, j, masked):
        st = st_ref[...]
        if masked:
            k_pos = j * tk + lax.broadcasted_iota(jnp.int32, (tk, 2 * tq), 0)
            col = lax.broadcasted_iota(jnp.int32, (tk, 2 * tq), 1)
            q_pos = q_pos0 + qi * tq + jnp.bitwise_and(col, tq - 1)
            visible = (lax.shift_right_logical(k_pos, 6) <= lax.shift_right_logical(q_pos, 6)) & (k_pos < kv_len)
            st = jnp.where(visible, st, NEG_INF)
        m_prev = m_ref[...]
        m_new = jnp.maximum(m_prev, jnp.max(st, axis=0, keepdims=True))
        alpha = jnp.exp2(m_prev - m_new)
        p = jnp.exp2(st - m_new)
        l_ref[...] = alpha * l_ref[...] + jnp.sum(p, axis=0, keepdims=True)
        pv = jnp.dot(vt_ref[j], p.astype(BF16), preferred_element_type=F32)
        acc_ref[...] = alpha * acc_ref[...] + pv
        m_ref[...] = m_new

    n_full = (q_pos0 + qi * tq) // tk
    scores_into(sta_ref, 0)
    bufs = (sta_ref, stb_ref)

    def run_groups(first, n_groups, group):
        def body(i, carry):
            base = first + group * i
            for u in range(group):
                scores_into(bufs[(u + 1) % 2], base + u + 1)
                consume(bufs[u % 2], base + u, False)
            return carry
        lax.fori_loop(0, n_groups, body, 0)

    n_quads = n_full // 4
    run_groups(0, n_quads, 4)
    run_groups(4 * n_quads, (n_full // 2) % 2, 2)
    for t in range(n_tail):
        if t + 1 < n_tail:
            scores_into(bufs[(t + 1) % 2], n_full + t + 1)
        consume(bufs[t % 2], n_full + t, True)

    o_ref[...] = _attn_finish(acc_ref[...] / l_ref[...], lam_ref[...], sg_ref[...], g_ref[...], tq,
                              lam_init).astype(o_ref.dtype)


def _attn_fixed_kernel(off_ref, lam_ref, qs_ref, k_ref, vt_ref, g_ref, sg_ref, o_ref, l_ref, acc_ref, sta_ref,
                       stb_ref, *, tq, tk, n_tail, lam_init):
    qi = pl.program_id(2)
    qs = qs_ref[...]
    offset = off_ref[...]

    l_ref[...] = jnp.zeros(l_ref.shape, F32)
    acc_ref[...] = jnp.zeros(acc_ref.shape, F32)

    def scores_into(st_ref, j):
        k = k_ref[pl.ds(pl.multiple_of(j * tk, tk), tk), :]
        st_ref[...] = lax.dot_general(k, qs, NT_DIMS, preferred_element_type=F32)

    def consume(st_ref, j, masked):
        st = st_ref[...] - offset
        if masked:
            k_pos = j * tk + lax.broadcasted_iota(jnp.int32, (tk, 2 * tq), 0)
            col = lax.broadcasted_iota(jnp.int32, (tk, 2 * tq), 1)
            q_pos = qi * tq + jnp.bitwise_and(col, tq - 1)
            visible = lax.shift_right_logical(k_pos, 6) <= lax.shift_right_logical(q_pos, 6)
            st = jnp.where(visible, st, NEG_INF)
        p = jnp.exp2(st)
        l_ref[...] += jnp.sum(p, axis=0, keepdims=True)
        acc_ref[...] += jnp.dot(vt_ref[j], p.astype(BF16), preferred_element_type=F32)

    n_full = (qi * tq) // tk
    scores_into(sta_ref, 0)
    bufs = (sta_ref, stb_ref)

    def run_groups(first, n_groups, group):
        def body(i, carry):
            base = first + group * i
            for u in range(group):
                scores_into(bufs[(u + 1) % 2], base + u + 1)
                consume(bufs[u % 2], base + u, False)
            return carry
        lax.fori_loop(0, n_groups, body, 0)

    n_quads = n_full // 4
    run_groups(0, n_quads, 4)
    run_groups(4 * n_quads, (n_full // 2) % 2, 2)
    for t in range(n_tail):
        if t + 1 < n_tail:
            scores_into(bufs[(t + 1) % 2], n_full + t + 1)
        consume(bufs[t % 2], n_full + t, True)

    o_ref[...] = _attn_finish(acc_ref[...] / l_ref[...], lam_ref[...], sg_ref[...], g_ref[...], tq,
                              lam_init).astype(o_ref.dtype)


def _attention(lam4, qs_bf, k_bf, vt_bf, proj, subln_col, score_bound, *, batch, t_q, t_k, tq, tk, q_pos0, kv_len,
               lam_init):
    nq = t_q // tq
    nkb = t_k // tk
    n_tail = max(1, tq // tk)
    assert t_k % tk == 0 and q_pos0 + t_q <= t_k and tq & (tq - 1) == 0
    assert (tq == 2 * tk and q_pos0 == 0) or (nq == 1 and nkb == 1)
    common = dict(
        out_shape=jax.ShapeDtypeStruct((batch * t_q, DA_WIDTH), BF16),
        grid=(batch, DA_HEADS, nq),
        out_specs=pl.BlockSpec((tq, LANES), lambda b, h, i: (b * nq + i, h)),
        compiler_params=_cparams(("parallel", "parallel", "arbitrary")),
    )
    in_specs = [
        pl.BlockSpec((4, DA_QK_DIM), lambda b, h, i: (0, 0)),
        pl.BlockSpec((None, 2 * tq, LANES), lambda b, h, i: (h, b * nq + i, 0)),
        pl.BlockSpec((None, None, t_k, LANES), lambda b, h, i: (h, b, 0, 0)),
        pl.BlockSpec((None, None, nkb, LANES, tk), lambda b, h, i: (h, b, 0, 0, 0)),
        pl.BlockSpec((None, tq, LANES), lambda b, h, i: (GAB0 + h, b * nq + i, 0)),
        pl.BlockSpec((LANES, 1), lambda b, h, i: (0, 0)),
    ]
    row = pltpu.VMEM((1, 2 * tq), F32)
    acc = pltpu.VMEM((LANES, 2 * tq), F32)
    st = pltpu.VMEM((tk, 2 * tq), F32)
    operands = (lam4, qs_bf, k_bf, vt_bf, proj, subln_col)

    def online(_):
        kern = functools.partial(_attn_kernel, tq=tq, tk=tk, n_tail=n_tail, q_pos0=q_pos0, kv_len=kv_len,
                                 lam_init=lam_init)
        return pl.pallas_call(kern, in_specs=in_specs, scratch_shapes=[row, row, acc, st, st],
                              name="diff_attention", **common)(*operands)

    def fixed(offset):
        kern = functools.partial(_attn_fixed_kernel, tq=tq, tk=tk, n_tail=n_tail, lam_init=lam_init)
        return pl.pallas_call(kern, in_specs=[pl.BlockSpec((1, 1), lambda b, h, i: (0, 0))] + in_specs,
                              scratch_shapes=[row, acc, st, st], name="diff_attention_fixed",
                              **common)(offset.reshape(1, 1), *operands)

    if score_bound is None:
        return online(None)
    return lax.cond(score_bound <= MAX_FIXED_OFFSET, fixed, online, score_bound)


def _rwkv_kernel(r_ref, k_ref, v_ref, g_ref, wa_ref, shift0_ref, wkv0_ref, mu_ref, w0_ref, a0_ref, kk_ref,
                 ka_ref, rk_ref, lng_ref, lnb_ref, wup_ref, aup_ref,
                 out_ref, state_ref, shift_ref,
                 h_ref, prev_ref, lhs_s, rhs_s, vm_s, ends_s, pc_s, bonus_s, *, c_len):
    c = pl.program_id(1)
    last = pl.num_programs(1) - 1
    cl = c_len
    heads = range(RW_HEADS)

    @pl.when(c == 0)
    def _():
        h_ref[...] = wkv0_ref[0]
        prev_ref[...] = shift0_ref[0]
        for s in (lhs_s, rhs_s, vm_s, ends_s, pc_s, bonus_s):
            s[...] = jnp.zeros(s.shape, s.dtype)

    row = lax.broadcasted_iota(jnp.int32, (cl, LANES), 0)
    lane = lax.broadcasted_iota(jnp.int32, (cl, LANES), 1)
    lo = lane < RW_HEAD
    own = (lo, jnp.logical_not(lo))
    sr = jnp.bitwise_and(lax.broadcasted_iota(jnp.int32, (2 * cl, 2 * cl), 0), cl - 1)
    sc = jnp.bitwise_and(lax.broadcasted_iota(jnp.int32, (2 * cl, 2 * cl), 1), cl - 1)
    top = lax.broadcasted_iota(jnp.int32, (2 * cl, 2 * cl), 0) < cl
    score_mask = sc < sr + jnp.where(top, 0, 1)
    tr = lax.broadcasted_iota(jnp.int32, (cl, cl), 0)
    tc = lax.broadcasted_iota(jnp.int32, (cl, cl), 1)
    tril_incl = (tc <= tr).astype(BF16)
    hr = lax.broadcasted_iota(jnp.int32, (RW_HEAD, LANES), 0)
    hl = lax.broadcasted_iota(jnp.int32, (RW_HEAD, LANES), 1)
    diag = (hl == hr, hl == hr + RW_HEAD)
    h_own = (hl < RW_HEAD, hl >= RW_HEAD)
    has_prev = (jnp.zeros((RW_HEAD, LANES), jnp.int32) + c) >= 1
    zeros_cb = jnp.zeros((cl, LANES), BF16)
    zeros_h = jnp.zeros((RW_HEAD, LANES), BF16)
    n_double = int(math.log2(cl))
    mm = functools.partial(jnp.dot, preferred_element_type=F32)

    lhs_prev = [lhs_s[hd] for hd in heads]
    vm_prev = [vm_s[hd] for hd in heads]
    rhs_prev = [rhs_s[p] for p in range(RW_BLOCKS)]
    ends_prev = [ends_s[p] for p in range(RW_BLOCKS)]
    pc_prev = [pc_s[p] for p in range(RW_BLOCKS)]
    bonus_prev = [bonus_s[p] for p in range(RW_BLOCKS)]

    def token_shift(x, blk):
        prev = jnp.where(row == 0, prev_ref[blk], pltpu.roll(x, 1, 0))
        return x + mu_ref[blk] * (prev - x)

    wa_raw = wa_ref[0]
    xs_wa = token_shift(wa_raw, 3 * RW_BLOCKS)
    w_lora = _bdot(jnp.tanh(xs_wa), wup_ref[...])
    a_lora = _bdot(xs_wa, aup_ref[...])
    xr, xk, xv, log_decay = [], [], [], []
    for p in range(RW_BLOCKS):
        r_raw, k_raw, v_raw = r_ref[p], k_ref[p], v_ref[p]
        xr.append(token_shift(r_raw, p))
        xk.append(token_shift(k_raw, RW_BLOCKS + p))
        xv.append(token_shift(v_raw, 2 * RW_BLOCKS + p))
        prev_ref[p] = r_raw[cl - 1:cl, :]
        prev_ref[RW_BLOCKS + p] = k_raw[cl - 1:cl, :]
        prev_ref[2 * RW_BLOCKS + p] = v_raw[cl - 1:cl, :]
        neg = -(w0_ref[p] + w_lora[:, p * LANES:(p + 1) * LANES])
        softplus = jnp.maximum(neg, 0.0) + jnp.log(1.0 + jnp.exp(-jnp.abs(neg)))
        log_decay.append(-jnp.exp(-softplus - 0.5))
    prev_ref[3 * RW_BLOCKS] = wa_raw[cl - 1:cl, :]
    ld_all = jnp.concatenate(log_decay, axis=1)
    ld_hi = ld_all.astype(BF16)
    ld_mid, ld_lo = _split2(ld_all - ld_hi.astype(F32))

    scores = [jnp.where(score_mask, lax.dot_general(lhs_prev[hd], rhs_prev[hd // 2], NT_DIMS,
                                                    preferred_element_type=F32), 0.0) for hd in heads]
    x = [lhs_prev[hd][0:cl, :].astype(F32)
         + mm(scores[hd][0:cl, :].astype(BF16), jnp.concatenate([zeros_cb, vm_prev[hd]], axis=0)) for hd in heads]
    cs_all = mm(tril_incl, ld_hi) + mm(tril_incl, ld_mid) + mm(tril_incl, ld_lo)
    a_pow = [scores[hd][0:cl, 0:cl] for hd in heads]
    for step in range(n_double):
        for hd in heads:
            if step + 1 < n_double:
                res = _bdot(a_pow[hd], jnp.concatenate([x[hd], a_pow[hd]], axis=1))
                x[hd] = x[hd] + res[:, 0:LANES]
                a_pow[hd] = res[:, LANES:LANES + cl]
            else:
                x[hd] = x[hd] + _bdot(a_pow[hd], x[hd])
    fin = []
    for hd in heads:
        p, hh = divmod(hd, 2)
        lhs_fin = jnp.concatenate([ends_prev[p][hh * RW_HEAD:(hh + 1) * RW_HEAD, :],
                                   scores[hd][cl:2 * cl, :].astype(BF16)], axis=0)
        fin.append(mm(lhs_fin, jnp.concatenate([x[hd].astype(BF16), vm_prev[hd]], axis=0)))
    y_m = []
    for hd in heads:
        p, hh = divmod(hd, 2)
        m_packed = fin[hd][0:RW_HEAD, :] + jnp.where(diag[hh], pc_prev[p], 0.0)
        g_packed = fin[hd][RW_HEAD:RW_HEAD + cl, :] + lhs_prev[hd][cl:2 * cl, :].astype(F32)
        h_old = h_ref[hd]
        h_hi, h_lo = _split2(h_old)
        pad = (lambda t: jnp.concatenate([t, zeros_h], axis=0)) if hh == 0 else (
            lambda t: jnp.concatenate([zeros_h, t], axis=0))
        h_hi, h_lo = pad(h_hi), pad(h_lo)
        y_m.append(mm(g_packed.astype(BF16), h_hi) + jnp.where(own[1 - hh], g_packed, 0.0))
        m_hi, m_lo = _split2(m_packed)
        h_new = mm(m_hi, h_hi) + mm(m_hi, h_lo) + mm(m_lo, h_hi) + jnp.where(h_own[1 - hh], m_packed, 0.0)
        h_ref[hd] = jnp.where(has_prev, h_new, h_old)

    inv_n = 1.0 / RW_HEAD
    for p in range(RW_BLOCKS):
        y = pltpu.roll(y_m[2 * p] + y_m[2 * p + 1], RW_HEAD, 1)
        yc = y - _group_sums(y, lo) * inv_n
        var = _group_sums(yc * yc, lo) * inv_n
        yn = yc * lax.rsqrt(var + LNX_EPS) * lng_ref[p] + lnb_ref[p]
        out_ref[:, p * LANES:(p + 1) * LANES] = ((yn + bonus_prev[p]) * _silu(g_ref[p])).astype(out_ref.dtype)

    for p in range(RW_BLOCKS):
        cols = slice(p * LANES, (p + 1) * LANES)
        a_sig = jax.nn.sigmoid(a0_ref[p] + a_lora[:, cols])
        kk_raw = xk[p] * kk_ref[p]
        kk = kk_raw * lax.rsqrt(jnp.maximum(_group_sums(kk_raw * kk_raw, lo), 1e-24))
        k_mod = xk[p] * (1.0 + (a_sig - 1.0) * ka_ref[p])
        b_vec = kk * a_sig
        cs = cs_all[:, cols]
        cs_last = cs[cl - 1:cl, :]
        decay_out = jnp.exp(-cs)
        decay_to_end = jnp.exp(cs_last - cs)
        a_t = -kk * jnp.exp(cs - log_decay[p])
        r_t = xr[p] * jnp.exp(cs)
        rhs_s[p] = jnp.concatenate([b_vec * decay_out, k_mod * decay_out], axis=0).astype(BF16)
        ends_s[p] = jnp.concatenate([b_vec * decay_to_end, k_mod * decay_to_end], axis=0).T.astype(BF16)
        pc_s[p] = jnp.exp(cs_last)
        bonus_s[p] = _group_sums(xr[p] * k_mod * rk_ref[p], lo) * xv[p]
        v_roll = pltpu.roll(xv[p], RW_HEAD, 1)
        for hh in range(2):
            hd = 2 * p + hh
            lhs_s[hd] = jnp.concatenate([jnp.where(own[hh], a_t, 0.0), jnp.where(own[hh], r_t, 0.0)],
                                        axis=0).astype(BF16)
            vm_s[hd] = jnp.where(own[1 - hh], v_roll, 0.0).astype(BF16)

    @pl.when(c == last)
    def _():
        state_ref[0] = h_ref[...]
        shift_ref[0] = prev_ref[...]


def _rwkv(proj, shift0_b, wkv0_p, prm, *, batch, t_len, c_len):
    nc = t_len // c_len
    n = batch * t_len
    kern = functools.partial(_rwkv_kernel, c_len=c_len)
    cur = lambda b, c: b * nc + jnp.minimum(c, nc - 1)
    prv = lambda b, c: b * nc + jnp.maximum(c - 1, 0)
    blk8 = lambda base, chunk: pl.BlockSpec((RW_BLOCKS, c_len, LANES),
                                            lambda b, c: (base // RW_BLOCKS, chunk(b, c), 0))
    vec8 = pl.BlockSpec((RW_BLOCKS, 1, LANES), lambda b, c: (0, 0, 0))
    lora = pl.BlockSpec((LANES, RW_WIDTH), lambda b, c: (0, 0))
    state_spec = pl.BlockSpec((1, RW_HEADS, RW_HEAD, LANES), lambda b, c: (b, 0, 0, 0))
    shift_spec = pl.BlockSpec((1, SHIFT_BLOCKS, 1, LANES), lambda b, c: (b, 0, 0, 0))
    return pl.pallas_call(
        kern,
        out_shape=(
            jax.ShapeDtypeStruct((n, RW_WIDTH), BF16),
            jax.ShapeDtypeStruct((batch, RW_HEADS, RW_HEAD, LANES), F32),
            jax.ShapeDtypeStruct((batch, SHIFT_BLOCKS, 1, LANES), F32),
        ),
        grid=(batch, nc + 1),
        in_specs=[
            blk8(RB0, cur), blk8(KRB0, cur), blk8(VRB0, cur), blk8(GRB0, prv),
            pl.BlockSpec((1, c_len, LANES), lambda b, c: (WAB0, cur(b, c), 0)),
            shift_spec, state_spec,
            pl.BlockSpec((SHIFT_BLOCKS, 1, LANES), lambda b, c: (0, 0, 0)),
            vec8, vec8, vec8, vec8, vec8, vec8, vec8, lora, lora,
        ],
        out_specs=(pl.BlockSpec((c_len, RW_WIDTH), lambda b, c: (prv(b, c), 0)), state_spec, shift_spec),
        scratch_shapes=[
            pltpu.VMEM((RW_HEADS, RW_HEAD, LANES), F32),
            pltpu.VMEM((SHIFT_BLOCKS, 1, LANES), F32),
            pltpu.VMEM((RW_HEADS, 2 * c_len, LANES), BF16),
            pltpu.VMEM((RW_BLOCKS, 2 * c_len, LANES), BF16),
            pltpu.VMEM((RW_HEADS, c_len, LANES), BF16),
            pltpu.VMEM((RW_BLOCKS, LANES, 2 * c_len), BF16),
            pltpu.VMEM((RW_BLOCKS, 1, LANES), F32),
            pltpu.VMEM((RW_BLOCKS, c_len, LANES), F32),
        ],
        compiler_params=_cparams(("parallel", "arbitrary")),
        name="rwkv7_chunked",
    )(proj, proj, proj, proj, proj, shift0_b, wkv0_p, prm["mu"], prm["w0"], prm["a0"], prm["k_k"], prm["k_a"],
      prm["r_k"], prm["lnx_g"], prm["lnx_b"], prm["w_up"], prm["a_up"])


def _pack_state(wkv):
    h = jnp.swapaxes(wkv, -1, -2)
    z = jnp.zeros_like(h)
    even = jnp.concatenate([z, h], axis=-1)
    odd = jnp.concatenate([h, z], axis=-1)
    is_even = (jnp.arange(RW_HEADS) % 2 == 0)[None, :, None, None]
    return jnp.where(is_even, even, odd)


def _unpack_state(hp):
    is_even = (jnp.arange(RW_HEADS) % 2 == 0)[None, :, None, None]
    h = jnp.where(is_even, hp[..., RW_HEAD:], hp[..., :RW_HEAD])
    return jnp.swapaxes(h, -1, -2)


def _out_proj_kernel(att_ref, rw_ref, wa_ref, wr_ref, x_ref, o_ref):
    acc = jnp.dot(att_ref[...], wa_ref[...], preferred_element_type=F32)
    acc = acc + jnp.dot(rw_ref[...], wr_ref[...], preferred_element_type=F32)
    o_ref[...] = x_ref[...] + acc


def _out_proj(att, rw, w_att, w_rw, x2, *, tm):
    n = x2.shape[0]
    return pl.pallas_call(
        _out_proj_kernel,
        out_shape=jax.ShapeDtypeStruct((n, D_MODEL), F32),
        grid=(n // tm,),
        in_specs=[
            pl.BlockSpec((tm, DA_WIDTH), lambda i: (i, 0)),
            pl.BlockSpec((tm, RW_WIDTH), lambda i: (i, 0)),
            pl.BlockSpec((DA_WIDTH, D_MODEL), lambda i: (0, 0)),
            pl.BlockSpec((RW_WIDTH, D_MODEL), lambda i: (0, 0)),
            pl.BlockSpec((tm, D_MODEL), lambda i: (i, 0)),
        ],
        out_specs=pl.BlockSpec((tm, D_MODEL), lambda i: (i, 0)),
        compiler_params=_cparams(("parallel",)),
        name="out_proj",
    )(att, rw, w_att, w_rw, x2)


def _blocks(v):
    return v.reshape(-1, 1, LANES)


def _prepare_params(norm_g, w_in, q_norm_g, k_norm_g, lq1, lk1, lq2, lk2, subln_g, shift_mu, w0, w_up, a0, a_up,
                    k_k, k_a, r_k, lnx_g, lnx_b, w_out):
    rw0 = 4 * DA_WIDTH
    lora0 = rw0 + 3 * RW_WIDTH
    g0 = rw0 + RW_SHIFT_COLS
    w_perm = jnp.concatenate([w_in[:, :lora0], w_in[:, g0:], w_in[:, lora0:g0],
                              jnp.zeros((D_MODEL, N_COL_BLOCKS * LANES - IN_COLS), w_in.dtype)], axis=1)
    half = DA_QK_DIM // 2
    inv_freq = ROPE_THETA ** (-jnp.arange(half, dtype=F32) / half)
    zeros = jnp.zeros((RW_LORA, RW_WIDTH), F32)
    return dict(
        norm_g=norm_g.reshape(1, D_MODEL),
        w_in=w_perm.astype(BF16),
        gq=jnp.tile(q_norm_g, 2).reshape(1, LANES),
        gk=jnp.tile(k_norm_g, 2).reshape(1, LANES),
        invf=jnp.tile(inv_freq, 4).reshape(1, LANES),
        lam4=jnp.stack([lq1, lk1, lq2, lk2]),
        score_bound=1.02 * LOG2_E * DA_QK_DIM ** 0.5 * jnp.max(jnp.abs(q_norm_g)) * jnp.max(jnp.abs(k_norm_g)),
        subln_col=subln_g.reshape(LANES, 1),
        mu=_blocks(shift_mu),
        w0=_blocks(w0), a0=_blocks(a0), k_k=_blocks(k_k), k_a=_blocks(k_a), r_k=_blocks(r_k.reshape(-1)),
        lnx_g=_blocks(lnx_g), lnx_b=_blocks(lnx_b),
        w_up=jnp.concatenate([w_up, zeros], axis=0).astype(BF16),
        a_up=jnp.concatenate([zeros, a_up], axis=0).astype(BF16),
        w_att=w_out[:DA_WIDTH].astype(BF16),
        w_rw=w_out[DA_WIDTH:].astype(BF16),
    )


def _tiles(batch, t_len, past_len):
    n = batch * t_len
    if past_len == 0:
        tq = min(t_len, 512)
        tk = tv = tq // 2
        t_k = t_len
    else:
        tq = tv = t_len
        t_k = tk = -(-(past_len + t_len) // LANES) * LANES
    return dict(tm=min(n, 1024), bn=6, tq=tq, tk=tk, tv=tv, t_k=t_k, c_len=min(t_len, CHUNK), tm_out=min(n, 512))


def _layer(x, past_k, past_v, wkv0, shift0, layer_idx, prm):
    batch, t_len, _ = x.shape
    past_len = past_k.shape[1]
    n = batch * t_len
    assert t_len & (t_len - 1) == 0
    tl = _tiles(batch, t_len, past_len)
    tq, tk, t_k = tl["tq"], tl["tk"], tl["t_k"]
    lam_init = 0.8 - 0.6 * math.exp(-0.3 * layer_idx)

    x2 = x.reshape(n, D_MODEL)
    proj = _in_proj(x2, prm["norm_g"], prm["w_in"], tm=tl["tm"], bn=tl["bn"])
    k_out, v_out, qs_bf, k_bf, vt_bf = _qkv_prep(proj, prm["gq"], prm["gk"], prm["invf"], batch=batch, t_len=t_len,
                                                 pos0=past_len, tq=tq, tv=tl["tv"])

    k_hm = k_bf.reshape(DA_HEADS, batch, t_len, LANES)
    if past_len:
        n_pad = t_k - past_len - t_len
        cache_hm = lambda c: c.reshape(batch, past_len, DA_HEADS, LANES).transpose(2, 0, 1, 3).astype(BF16)
        k_hm = jnp.concatenate([cache_hm(past_k), k_hm, jnp.zeros((DA_HEADS, batch, n_pad, LANES), BF16)], axis=2)
        vt_bf = jnp.concatenate([jnp.swapaxes(cache_hm(past_v), 2, 3)[:, :, None], vt_bf,
                                 jnp.zeros((DA_HEADS, batch, 1, LANES, n_pad), BF16)], axis=4)

    att = _attention(prm["lam4"], qs_bf, k_hm, vt_bf, proj, prm["subln_col"],
                     None if past_len else prm["score_bound"], batch=batch, t_q=t_len, t_k=t_k, tq=tq, tk=tk,
                     q_pos0=past_len, kv_len=past_len + t_len, lam_init=lam_init)

    shift0_b = shift0.reshape(batch, SHIFT_BLOCKS, 1, LANES)
    rw, state_p, shift_b = _rwkv(proj, shift0_b, _pack_state(wkv0), prm, batch=batch, t_len=t_len,
                                 c_len=tl["c_len"])

    y = _out_proj(att, rw, prm["w_att"], prm["w_rw"], x2, tm=tl["tm_out"])
    return (y.reshape(batch, t_len, D_MODEL),
            k_out.reshape(batch, t_len, DA_HEADS, 2, DA_QK_DIM),
            v_out.reshape(batch, t_len, DA_HEADS, DA_V_DIM),
            _unpack_state(state_p),
            shift_b.reshape(batch, 1, RW_SHIFT_COLS))


def kernel(x_prompt, x_sample, cache_attn_k, cache_attn_v, state_rwkv_wkv, state_rwkv_shift, norm_g, w_in, q_norm_g, k_norm_g, lambda_q1, lambda_k1, lambda_q2, lambda_k2, subln_g, shift_mu, w0, w_up, a0, a_up, k_k, k_a, r_k, lnx_g, lnx_b, w_out):
    depth = w_in.shape[0]
    batch = x_prompt.shape[0]
    dt = x_prompt.dtype
    yp, ys = x_prompt, x_sample
    outs = [[] for _ in range(8)]
    for l in range(depth):
        prm = _prepare_params(norm_g[l], w_in[l], q_norm_g[l], k_norm_g[l], lambda_q1[l], lambda_k1[l], lambda_q2[l],
                              lambda_k2[l], subln_g[l], shift_mu[l], w0[l], w_up[l], a0[l], a_up[l], k_k[l], k_a[l],
                              r_k[l], lnx_g[l], lnx_b[l], w_out[l])
        yp, kp, vp, sp, hp = _layer(
            yp, jnp.zeros((batch, 0, DA_HEADS, 2, DA_QK_DIM), dt), jnp.zeros((batch, 0, DA_HEADS, DA_V_DIM), dt),
            jnp.zeros((batch, RW_HEADS, RW_HEAD, RW_HEAD), dt), jnp.zeros((batch, 1, RW_SHIFT_COLS), dt), l, prm)
        ys, ksn, vsn, ssn, hsn = _layer(ys, cache_attn_k[l], cache_attn_v[l], state_rwkv_wkv[l],
                                        state_rwkv_shift[l], l, prm)
        for lst, val in zip(outs, (kp, vp, sp, hp, ksn, vsn, ssn, hsn)):
            lst.append(val)
    return (yp, ys) + tuple(jnp.stack(o) for o in outs)
```

```python
import functools
import math

import jax
import jax.numpy as jnp
from jax import lax
from jax.experimental import pallas as pl
from jax.experimental.pallas import tpu as pltpu

F32 = jnp.float32
BF16 = jnp.bfloat16

D_MODEL = 2048
CHUNK = 64
DA_HEADS = 8
DA_V_DIM = 128
DA_QK_DIM = 64
DA_WIDTH = DA_HEADS * DA_V_DIM
RW_WIDTH = 1024
RW_HEAD = 64
RW_HEADS = RW_WIDTH // RW_HEAD
RW_LORA = 64
RW_SHIFT_COLS = 3 * RW_WIDTH + 2 * RW_LORA
IN_COLS = 4 * DA_WIDTH + RW_SHIFT_COLS + RW_WIDTH
ROPE_THETA = 10000.0
NORM_EPS = 1e-6
LNX_EPS = 64e-5
NEG_INF = -1e30
LOG2_E = math.log2(math.e)
MAX_FIXED_OFFSET = 60.0

LANES = 128
MXU_COLS = 256
N_COL_BLOCKS = -(-IN_COLS // MXU_COLS) * (MXU_COLS // LANES)
QB0, KB0, VB0, GAB0, RB0, KRB0, VRB0, GRB0, WAB0 = 0, 8, 16, 24, 32, 40, 48, 56, 64
RW_BLOCKS = RW_WIDTH // LANES
SHIFT_BLOCKS = RW_SHIFT_COLS // LANES

VMEM_LIMIT = 56 * 1024 * 1024

NT_DIMS = (((1,), (1,)), ((), ()))


def _cparams(sem):
    return pltpu.CompilerParams(dimension_semantics=sem, vmem_limit_bytes=VMEM_LIMIT)


def _bdot(a, b):
    return jnp.dot(a.astype(BF16), b.astype(BF16), preferred_element_type=F32)


def _split2(x):
    hi = x.astype(BF16)
    return hi, (x - hi.astype(F32)).astype(BF16)


def _group_sums(x, lo):
    s_lo = jnp.sum(jnp.where(lo, x, 0.0), axis=-1, keepdims=True)
    s_hi = jnp.sum(jnp.where(lo, 0.0, x), axis=-1, keepdims=True)
    return jnp.where(lo, s_lo, s_hi)


def _silu(g):
    return g * jax.nn.sigmoid(g)


def _in_proj_kernel(x_ref, g_ref, w_ref, o_ref, xn_ref):
    @pl.when(pl.program_id(1) == 0)
    def _():
        xf = x_ref[...]
        ms = jnp.mean(xf * xf, axis=-1, keepdims=True)
        xn_ref[...] = (xf * lax.rsqrt(ms + NORM_EPS) * g_ref[...]).astype(BF16)

    acc = jnp.dot(xn_ref[...], w_ref[...], preferred_element_type=F32)
    for c in range(o_ref.shape[0]):
        o_ref[c] = acc[:, c * LANES:(c + 1) * LANES]


def _in_proj(x2, norm_g, w_bf, *, tm, bn):
    n = x2.shape[0]
    return pl.pallas_call(
        _in_proj_kernel,
        out_shape=jax.ShapeDtypeStruct((N_COL_BLOCKS, n, LANES), F32),
        grid=(n // tm, N_COL_BLOCKS // bn),
        in_specs=[
            pl.BlockSpec((tm, D_MODEL), lambda i, j: (i, 0)),
            pl.BlockSpec((1, D_MODEL), lambda i, j: (0, 0)),
            pl.BlockSpec((D_MODEL, bn * LANES), lambda i, j: (0, j)),
        ],
        out_specs=pl.BlockSpec((bn, tm, LANES), lambda i, j: (j, i, 0)),
        scratch_shapes=[pltpu.VMEM((tm, D_MODEL), BF16)],
        compiler_params=_cparams(("parallel", "arbitrary")),
        name="in_proj",
    )(x2, norm_g, w_bf)


def _qkv_kernel(p_ref, gq_ref, gk_ref, invf_ref, kout_ref, vout_ref, qs_ref, kbf_ref, vt_ref, *, tq, tv, pos0,
                k_time_minor):
    ti = pl.program_id(1)
    row = lax.broadcasted_iota(jnp.int32, (tq, LANES), 0)
    lane = lax.broadcasted_iota(jnp.int32, (tq, LANES), 1)
    ang = (pos0 + ti * tq + row).astype(F32) * invf_ref[...]
    cos = jnp.cos(ang)
    sin = jnp.sin(ang)
    first_half = jnp.bitwise_and(lane, DA_QK_DIM - 1) < DA_QK_DIM // 2
    sin_signed = jnp.where(first_half, -sin, sin)
    lo = lane < DA_QK_DIM

    def norm_rope(x, g):
        ms = _group_sums(x * x, lo) * (1.0 / DA_QK_DIM)
        xn = x * lax.rsqrt(ms + NORM_EPS) * g
        rot = jnp.where(first_half, pltpu.roll(xn, LANES - 32, 1), pltpu.roll(xn, 32, 1))
        return xn * cos + rot * sin_signed

    gq = gq_ref[...]
    gk = gk_ref[...]
    for h in range(DA_HEADS):
        q = norm_rope(p_ref[QB0 + h], gq) * (LOG2_E * DA_QK_DIM ** -0.5)
        qs_ref[h, 0:tq, :] = jnp.where(lo, q, 0.0).astype(BF16)
        qs_ref[h, tq:2 * tq, :] = jnp.where(lo, 0.0, q).astype(BF16)
        k = norm_rope(p_ref[KB0 + h], gk)
        if k_time_minor:
            kt = k.T
            kout_ref[h, 0] = kt[0:DA_QK_DIM, :]
            kout_ref[h, 1] = kt[DA_QK_DIM:2 * DA_QK_DIM, :]
        else:
            kout_ref[:, h * LANES:(h + 1) * LANES] = k
        kbf_ref[h] = k.astype(BF16)
        v = p_ref[VB0 + h]
        vout_ref[:, h * LANES:(h + 1) * LANES] = v
        for s in range(tq // tv):
            vt_ref[h, s] = v[s * tv:(s + 1) * tv, :].T.astype(BF16)


def _qkv_prep(proj, gq, gk, invf, *, batch, t_len, pos0, tq, tv, k_time_minor):
    n = batch * t_len
    nq = t_len // tq
    kern = functools.partial(_qkv_kernel, tq=tq, tv=tv, pos0=pos0, k_time_minor=k_time_minor)
    row_major = jax.ShapeDtypeStruct((n, DA_WIDTH), F32)
    rm_spec = pl.BlockSpec((tq, DA_WIDTH), lambda b, t: (b * nq + t, 0))
    vec_spec = pl.BlockSpec((1, LANES), lambda b, t: (0, 0))
    if k_time_minor:
        k_shape = jax.ShapeDtypeStruct((batch, DA_HEADS, 2, DA_QK_DIM, t_len), F32)
        k_spec = pl.BlockSpec((None, DA_HEADS, 2, DA_QK_DIM, tq), lambda b, t: (b, 0, 0, 0, t))
    else:
        k_shape, k_spec = row_major, rm_spec
    return pl.pallas_call(
        kern,
        out_shape=(k_shape, row_major,
                   jax.ShapeDtypeStruct((DA_HEADS, 2 * n, LANES), BF16),
                   jax.ShapeDtypeStruct((DA_HEADS, n, LANES), BF16),
                   jax.ShapeDtypeStruct((DA_HEADS, batch, t_len // tv, LANES, tv), BF16)),
        grid=(batch, nq),
        in_specs=[pl.BlockSpec((3 * DA_HEADS, tq, LANES), lambda b, t: (0, b * nq + t, 0)),
                  vec_spec, vec_spec, vec_spec],
        out_specs=(k_spec, rm_spec,
                   pl.BlockSpec((DA_HEADS, 2 * tq, LANES), lambda b, t: (0, b * nq + t, 0)),
                   pl.BlockSpec((DA_HEADS, tq, LANES), lambda b, t: (0, b * nq + t, 0)),
                   pl.BlockSpec((DA_HEADS, None, tq // tv, LANES, tv), lambda b, t: (0, b, t, 0, 0))),
        compiler_params=_cparams(("parallel", "parallel")),
        name="qkv_prep",
    )(proj, gq, gk, invf)


def _attn_finish(on, lam4, sg_col, g, tq, lam_init):
    lam = (jnp.exp(jnp.sum(lam4[0:1] * lam4[1:2], axis=-1, keepdims=True))
           - jnp.exp(jnp.sum(lam4[2:3] * lam4[3:4], axis=-1, keepdims=True)) + lam_init)
    ot = on[:, 0:tq] - lam * on[:, tq:2 * tq]
    ms = jnp.mean(ot * ot, axis=0, keepdims=True)
    ot = ot * lax.rsqrt(ms + NORM_EPS) * sg_col * (1.0 - lam_init)
    return ot.T * _silu(g)


def _attn_kernel(lam_ref, qs_ref, k_ref, vt_ref, g_ref, sg_ref, o_ref, m_ref, l_ref, acc_ref, sta_ref, stb_ref,
                 *, tq, tk, n_tail, q_pos0, kv_len, lam_init):
    qi = pl.program_id(2)
    qs = qs_ref[...]

    m_ref[...] = jnp.full(m_ref.shape, NEG_INF, F32)
    l_ref[...] = jnp.zeros(l_ref.shape, F32)
    acc_ref[...] = jnp.zeros(acc_ref.shape, F32)

    def scores_into(st_ref, j):
        k = k_ref[pl.ds(pl.multiple_of(j * tk, tk), tk), :]
        st_ref[...] = lax.dot_general(k, qs, NT_DIMS, preferred_element_type=F32)

    def consume(st_ref, j, masked):
        st = st_ref[...]
        if masked:
            k_pos = j * tk + lax.broadcasted_iota(jnp.int32, (tk, 2 * tq), 0)
            col = lax.broadcasted_iota(jnp.int32, (tk, 2 * tq), 1)
            q_pos = q_pos0 + qi * tq + jnp.bitwise_and(col, tq - 1)
            visible = (lax.shift_right_logical(k_pos, 6) <= lax.shift_right_logical(q_pos, 6)) & (k_pos < kv_len)
            st = jnp.where(visible, st, NEG_INF)
        m_prev = m_ref[...]
        m_new = jnp.maximum(m_prev, jnp.max(st, axis=0, keepdims=True))
        alpha = jnp.exp2(m_prev - m_new)
        p = jnp.exp2(st - m_new)
        l_ref[...] = alpha * l_ref[...] + jnp.sum(p, axis=0, keepdims=True)
        pv = jnp.dot(vt_ref[j], p.astype(BF16), preferred_element_type=F32)
        acc_ref[...] = alpha * acc_ref[...] + pv
        m_ref[...] = m_new

    n_full = (q_pos0 + qi * tq) // tk
    scores_into(sta_ref, 0)
    bufs = (sta_ref, stb_ref)

    def run_groups(first, n_groups, group):
        def body(i, carry):
            base = first + group * i
            for u in range(group):
                scores_into(bufs[(u + 1) % 2], base + u + 1)
                consume(bufs[u % 2], base + u, False)
            return carry
        lax.fori_loop(0, n_groups, body, 0)

    n_quads = n_full // 4
    run_groups(0, n_quads, 4)
    run_groups(4 * n_quads, (n_full // 2) % 2, 2)
    for t in range(n_tail):
        if t + 1 < n_tail:
            scores_into(bufs[(t + 1) % 2], n_full + t + 1)
        consume(bufs[t % 2], n_full + t, True)

    o_ref[...] = _attn_finish(acc_ref[...] / l_ref[...], lam_ref[...], sg_ref[...], g_ref[...], tq,
                              lam_init).astype(o_ref.dtype)


def _attn_fixed_kernel(off_ref, lam_ref, qs_ref, k_ref, vt_ref, g_ref, sg_ref, o_ref, l_ref, acc_ref, sta_ref,
                       stb_ref, *, tq, tk, n_tail, lam_init):
    qi = pl.program_id(2)
    qs = qs_ref[...]
    offset = off_ref[...]

    l_ref[...] = jnp.zeros(l_ref.shape, F32)
    acc_ref[...] = jnp.zeros(acc_ref.shape, F32)

    def scores_into(st_ref, j):
        k = k_ref[pl.ds(pl.multiple_of(j * tk, tk), tk), :]
        st_ref[...] = lax.dot_general(k, qs, NT_DIMS, preferred_element_type=F32)

    def consume(st_ref, j, masked):
        st = st_ref[...] - offset
        if masked:
            k_pos = j * tk + lax.broadcasted_iota(jnp.int32, (tk, 2 * tq), 0)
            col = lax.broadcasted_iota(jnp.int32, (tk, 2 * tq), 1)
            q_pos = qi * tq + jnp.bitwise_and(col, tq - 1)
            visible = lax.shift_right_logical(k_pos, 6) <= lax.shift_right_logical(q_pos, 6)
            st = jnp.where(visible, st, NEG_INF)
        p = jnp.exp2(st)
        l_ref[...] += jnp.sum(p, axis=0, keepdims=True)
        acc_ref[...] += jnp.dot(vt_ref[j], p.astype(BF16), preferred_element_type=F32)

    n_full = (qi * tq) // tk
    scores_into(sta_ref, 0)
    bufs = (sta_ref, stb_ref)

    def run_groups(first, n_groups, group):
        def body(i, carry):
            base = first + group * i
            for u in range(group):
                scores_into(bufs[(u + 1) % 2], base + u + 1)
                consume(bufs[u % 2], base + u, False)
            return carry
        lax.fori_loop(0, n_groups, body, 0)

    n_quads = n_full // 4
    run_groups(0, n_quads, 4)
    run_groups(4 * n_quads, (n_full // 2) % 2, 2)
    for t in range(n_tail):
        if t + 1 < n_tail:
            scores_into(bufs[(t + 1) % 2], n_full + t + 1)
        consume(bufs[t % 2], n_full + t, True)

    o_ref[...] = _attn_finish(acc_ref[...] / l_ref[...], lam_ref[...], sg_ref[...], g_ref[...], tq,
                              lam_init).astype(o_ref.dtype)


def _attention(lam4, qs_bf, k_bf, vt_bf, proj, subln_col, score_bound, *, batch, t_q, t_k, tq, tk, q_pos0, kv_len,
               lam_init):
    nq = t_q // tq
    nkb = t_k // tk
    n_tail = max(1, tq // tk)
    assert t_k % tk == 0 and q_pos0 + t_q <= t_k and tq & (tq - 1) == 0
    assert (tq == 2 * tk and q_pos0 == 0) or (nq == 1 and nkb == 1)
    common = dict(
        out_shape=jax.ShapeDtypeStruct((batch * t_q, DA_WIDTH), BF16),
        grid=(batch, DA_HEADS, nq),
        out_specs=pl.BlockSpec((tq, LANES), lambda b, h, i: (b * nq + i, h)),
        compiler_params=_cparams(("parallel", "parallel", "arbitrary")),
    )
    in_specs = [
        pl.BlockSpec((4, DA_QK_DIM), lambda b, h, i: (0, 0)),
        pl.BlockSpec((None, 2 * tq, LANES), lambda b, h, i: (h, b * nq + i, 0)),
        pl.BlockSpec((None, None, t_k, LANES), lambda b, h, i: (h, b, 0, 0)),
        pl.BlockSpec((None, None, nkb, LANES, tk), lambda b, h, i: (h, b, 0, 0, 0)),
        pl.BlockSpec((None, tq, LANES), lambda b, h, i: (GAB0 + h, b * nq + i, 0)),
        pl.BlockSpec((LANES, 1), lambda b, h, i: (0, 0)),
    ]
    row = pltpu.VMEM((1, 2 * tq), F32)
    acc = pltpu.VMEM((LANES, 2 * tq), F32)
    st = pltpu.VMEM((tk, 2 * tq), F32)
    operands = (lam4, qs_bf, k_bf, vt_bf, proj, subln_col)

    def online(_):
        kern = functools.partial(_attn_kernel, tq=tq, tk=tk, n_tail=n_tail, q_pos0=q_pos0, kv_len=kv_len,
                                 lam_init=lam_init)
        return pl.pallas_call(kern, in_specs=in_specs, scratch_shapes=[row, row, acc, st, st],
                              name="diff_attention", **common)(*operands)

    def fixed(offset):
        kern = functools.partial(_attn_fixed_kernel, tq=tq, tk=tk, n_tail=n_tail, lam_init=lam_init)
        return pl.pallas_call(kern, in_specs=[pl.BlockSpec((1, 1), lambda b, h, i: (0, 0))] + in_specs,
                              scratch_shapes=[row, acc, st, st], name="diff_attention_fixed",
                              **common)(offset.reshape(1, 1), *operands)

    if score_bound is None:
        return online(None)
    return lax.cond(score_bound <= MAX_FIXED_OFFSET, fixed, online, score_bound)


def _attn_decode_kernel(lam_ref, qs_ref, kt_ref, vc_ref, kn_ref, vn_ref, g_ref, sg_ref, o_ref, *, tq, past_len,
                        lam_init):
    lam4 = lam_ref[...]
    lam = (jnp.exp(jnp.sum(lam4[0:1] * lam4[1:2], axis=-1, keepdims=True))
           - jnp.exp(jnp.sum(lam4[2:3] * lam4[3:4], axis=-1, keepdims=True)) + lam_init)
    v_new = vn_ref[...]
    for h in range(DA_HEADS):
        qs = qs_ref[h]
        s_c = jnp.dot(qs, kt_ref[h].astype(BF16), preferred_element_type=F32)
        s_n = lax.dot_general(qs, kn_ref[h], NT_DIMS, preferred_element_type=F32)
        m = jnp.maximum(jnp.max(s_c, axis=-1, keepdims=True), jnp.max(s_n, axis=-1, keepdims=True))
        p_c = jnp.exp2(s_c - m)
        p_n = jnp.exp2(s_n - m)
        l = jnp.sum(p_c, axis=-1, keepdims=True) + jnp.sum(p_n, axis=-1, keepdims=True)
        v_c = vc_ref[pl.ds(h, past_len, stride=DA_HEADS), :]
        acc = (_bdot(p_c, v_c) + _bdot(p_n, v_new[:, h * LANES:(h + 1) * LANES])) / l
        o = acc[0:tq, :] - lam * acc[tq:2 * tq, :]
        ms = jnp.mean(o * o, axis=-1, keepdims=True)
        o = o * lax.rsqrt(ms + NORM_EPS) * sg_ref[...] * (1.0 - lam_init)
        o_ref[:, h * LANES:(h + 1) * LANES] = (o * _silu(g_ref[h])).astype(o_ref.dtype)


def _attention_decode(lam4, qs_bf, cache_kt, cache_v, k_bf, v_rows, proj, subln_row, *, batch, t_q, lam_init):
    past_len = cache_kt.shape[-1]
    assert past_len % CHUNK == 0 and t_q <= CHUNK
    kern = functools.partial(_attn_decode_kernel, tq=t_q, past_len=past_len, lam_init=lam_init)
    return pl.pallas_call(
        kern,
        out_shape=jax.ShapeDtypeStruct((batch * t_q, DA_WIDTH), BF16),
        grid=(batch,),
        in_specs=[
            pl.BlockSpec((4, DA_QK_DIM), lambda b: (0, 0)),
            pl.BlockSpec((DA_HEADS, 2 * t_q, LANES), lambda b: (0, b, 0)),
            pl.BlockSpec((None, DA_HEADS, LANES, past_len), lambda b: (b, 0, 0, 0)),
            pl.BlockSpec((None, past_len * DA_HEADS, LANES), lambda b: (b, 0, 0)),
            pl.BlockSpec((DA_HEADS, t_q, LANES), lambda b: (0, b, 0)),
            pl.BlockSpec((t_q, DA_WIDTH), lambda b: (b, 0)),
            pl.BlockSpec((DA_HEADS, t_q, LANES), lambda b: (GAB0 // DA_HEADS, b, 0)),
            pl.BlockSpec((1, LANES), lambda b: (0, 0)),
        ],
        out_specs=pl.BlockSpec((t_q, DA_WIDTH), lambda b: (b, 0)),
        compiler_params=_cparams(("parallel",)),
        name="diff_attention_decode",
    )(lam4, qs_bf, cache_kt, cache_v, k_bf, v_rows, proj, subln_row)


def _rwkv_kernel(r_ref, k_ref, v_ref, g_ref, wa_ref, shift0_ref, wkv0_ref, mu_ref, w0_ref, a0_ref, kk_ref,
                 ka_ref, rk_ref, lng_ref, lnb_ref, wup_ref, aup_ref,
                 out_ref, state_ref, shift_ref,
                 h_ref, prev_ref, lhs_s, rhs_s, vm_s, ends_s, pc_s, bonus_s, *, c_len):
    c = pl.program_id(1)
    last = pl.num_programs(1) - 1
    cl = c_len
    heads = range(RW_HEADS)

    @pl.when(c == 0)
    def _():
        h_ref[...] = wkv0_ref[0]
        prev_ref[...] = shift0_ref[0]
        for s in (lhs_s, rhs_s, vm_s, ends_s, pc_s, bonus_s):
            s[...] = jnp.zeros(s.shape, s.dtype)

    row = lax.broadcasted_iota(jnp.int32, (cl, LANES), 0)
    lane = lax.broadcasted_iota(jnp.int32, (cl, LANES), 1)
    lo = lane < RW_HEAD
    own = (lo, jnp.logical_not(lo))
    sr = jnp.bitwise_and(lax.broadcasted_iota(jnp.int32, (2 * cl, 2 * cl), 0), cl - 1)
    sc = jnp.bitwise_and(lax.broadcasted_iota(jnp.int32, (2 * cl, 2 * cl), 1), cl - 1)
    top = lax.broadcasted_iota(jnp.int32, (2 * cl, 2 * cl), 0) < cl
    score_mask = sc < sr + jnp.where(top, 0, 1)
    tr = lax.broadcasted_iota(jnp.int32, (cl, cl), 0)
    tc = lax.broadcasted_iota(jnp.int32, (cl, cl), 1)
    tril_incl = (tc <= tr).astype(BF16)
    hr = lax.broadcasted_iota(jnp.int32, (RW_HEAD, LANES), 0)
    hl = lax.broadcasted_iota(jnp.int32, (RW_HEAD, LANES), 1)
    diag = (hl == hr, hl == hr + RW_HEAD)
    h_own = (hl < RW_HEAD, hl >= RW_HEAD)
    has_prev = (jnp.zeros((RW_HEAD, LANES), jnp.int32) + c) >= 1
    zeros_cb = jnp.zeros((cl, LANES), BF16)
    zeros_h = jnp.zeros((RW_HEAD, LANES), BF16)
    n_double = int(math.log2(cl))
    mm = functools.partial(jnp.dot, preferred_element_type=F32)

    lhs_prev = [lhs_s[hd] for hd in heads]
    vm_prev = [vm_s[hd] for hd in heads]
    rhs_prev = [rhs_s[p] for p in range(RW_BLOCKS)]
    ends_prev = [ends_s[p] for p in range(RW_BLOCKS)]
    pc_prev = [pc_s[p] for p in range(RW_BLOCKS)]
    bonus_prev = [bonus_s[p] for p in range(RW_BLOCKS)]

    def token_shift(x, blk):
        prev = jnp.where(row == 0, prev_ref[blk], pltpu.roll(x, 1, 0))
        return x + mu_ref[blk] * (prev - x)

    wa_raw = wa_ref[0]
    xs_wa = token_shift(wa_raw, 3 * RW_BLOCKS)
    w_lora = _bdot(jnp.tanh(xs_wa), wup_ref[...])
    a_lora = _bdot(xs_wa, aup_ref[...])
    xr, xk, xv, log_decay = [], [], [], []
    for p in range(RW_BLOCKS):
        r_raw, k_raw, v_raw = r_ref[p], k_ref[p], v_ref[p]
        xr.append(token_shift(r_raw, p))
        xk.append(token_shift(k_raw, RW_BLOCKS + p))
        xv.append(token_shift(v_raw, 2 * RW_BLOCKS + p))
        prev_ref[p] = r_raw[cl - 1:cl, :]
        prev_ref[RW_BLOCKS + p] = k_raw[cl - 1:cl, :]
        prev_ref[2 * RW_BLOCKS + p] = v_raw[cl - 1:cl, :]
        neg = -(w0_ref[p] + w_lora[:, p * LANES:(p + 1) * LANES])
        softplus = jnp.maximum(neg, 0.0) + jnp.log(1.0 + jnp.exp(-jnp.abs(neg)))
        log_decay.append(-jnp.exp(-softplus - 0.5))
    prev_ref[3 * RW_BLOCKS] = wa_raw[cl - 1:cl, :]
    ld_all = jnp.concatenate(log_decay, axis=1)
    ld_hi = ld_all.astype(BF16)
    ld_mid, ld_lo = _split2(ld_all - ld_hi.astype(F32))

    scores = [jnp.where(score_mask, lax.dot_general(lhs_prev[hd], rhs_prev[hd // 2], NT_DIMS,
                                                    preferred_element_type=F32), 0.0) for hd in heads]
    x = [lhs_prev[hd][0:cl, :].astype(F32)
         + mm(scores[hd][0:cl, :].astype(BF16), jnp.concatenate([zeros_cb, vm_prev[hd]], axis=0)) for hd in heads]
    cs_all = mm(tril_incl, ld_hi) + mm(tril_incl, ld_mid) + mm(tril_incl, ld_lo)
    a_pow = [scores[hd][0:cl, 0:cl] for hd in heads]
    for step in range(n_double):
        for hd in heads:
            if step + 1 < n_double:
                res = _bdot(a_pow[hd], jnp.concatenate([x[hd], a_pow[hd]], axis=1))
                x[hd] = x[hd] + res[:, 0:LANES]
                a_pow[hd] = res[:, LANES:LANES + cl]
            else:
                x[hd] = x[hd] + _bdot(a_pow[hd], x[hd])
    fin = []
    for hd in heads:
        p, hh = divmod(hd, 2)
        lhs_fin = jnp.concatenate([ends_prev[p][hh * RW_HEAD:(hh + 1) * RW_HEAD, :],
                                   scores[hd][cl:2 * cl, :].astype(BF16)], axis=0)
        fin.append(mm(lhs_fin, jnp.concatenate([x[hd].astype(BF16), vm_prev[hd]], axis=0)))
    y_m = []
    for hd in heads:
        p, hh = divmod(hd, 2)
        m_packed = fin[hd][0:RW_HEAD, :] + jnp.where(diag[hh], pc_prev[p], 0.0)
        g_packed = fin[hd][RW_HEAD:RW_HEAD + cl, :] + lhs_prev[hd][cl:2 * cl, :].astype(F32)
        h_old = h_ref[hd]
        h_hi, h_lo = _split2(h_old)
        pad = (lambda t: jnp.concatenate([t, zeros_h], axis=0)) if hh == 0 else (
            lambda t: jnp.concatenate([zeros_h, t], axis=0))
        h_hi, h_lo = pad(h_hi), pad(h_lo)
        y_m.append(mm(g_packed.astype(BF16), h_hi) + jnp.where(own[1 - hh], g_packed, 0.0))
        m_hi, m_lo = _split2(m_packed)
        h_new = mm(m_hi, h_hi) + mm(m_hi, h_lo) + mm(m_lo, h_hi) + jnp.where(h_own[1 - hh], m_packed, 0.0)
        h_ref[hd] = jnp.where(has_prev, h_new, h_old)

    inv_n = 1.0 / RW_HEAD
    for p in range(RW_BLOCKS):
        y = pltpu.roll(y_m[2 * p] + y_m[2 * p + 1], RW_HEAD, 1)
        yc = y - _group_sums(y, lo) * inv_n
        var = _group_sums(yc * yc, lo) * inv_n
        yn = yc * lax.rsqrt(var + LNX_EPS) * lng_ref[p] + lnb_ref[p]
        out_ref[:, p * LANES:(p + 1) * LANES] = ((yn + bonus_prev[p]) * _silu(g_ref[p])).astype(out_ref.dtype)

    for p in range(RW_BLOCKS):
        cols = slice(p * LANES, (p + 1) * LANES)
        a_sig = jax.nn.sigmoid(a0_ref[p] + a_lora[:, cols])
        kk_raw = xk[p] * kk_ref[p]
        kk = kk_raw * lax.rsqrt(jnp.maximum(_group_sums(kk_raw * kk_raw, lo), 1e-24))
        k_mod = xk[p] * (1.0 + (a_sig - 1.0) * ka_ref[p])
        b_vec = kk * a_sig
        cs = cs_all[:, cols]
        cs_last = cs[cl - 1:cl, :]
        decay_out = jnp.exp(-cs)
        decay_to_end = jnp.exp(cs_last - cs)
        a_t = -kk * jnp.exp(cs - log_decay[p])
        r_t = xr[p] * jnp.exp(cs)
        rhs_s[p] = jnp.concatenate([b_vec * decay_out, k_mod * decay_out], axis=0).astype(BF16)
        ends_s[p] = jnp.concatenate([b_vec * decay_to_end, k_mod * decay_to_end], axis=0).T.astype(BF16)
        pc_s[p] = jnp.exp(cs_last)
        bonus_s[p] = _group_sums(xr[p] * k_mod * rk_ref[p], lo) * xv[p]
        v_roll = pltpu.roll(xv[p], RW_HEAD, 1)
        for hh in range(2):
            hd = 2 * p + hh
            lhs_s[hd] = jnp.concatenate([jnp.where(own[hh], a_t, 0.0), jnp.where(own[hh], r_t, 0.0)],
                                        axis=0).astype(BF16)
            vm_s[hd] = jnp.where(own[1 - hh], v_roll, 0.0).astype(BF16)

    @pl.when(c == last)
    def _():
        state_ref[0] = h_ref[...]
        shift_ref[0] = prev_ref[...]


def _rwkv(proj, shift0_b, wkv0_p, prm, *, batch, t_len, c_len):
    nc = t_len // c_len
    n = batch * t_len
    kern = functools.partial(_rwkv_kernel, c_len=c_len)
    cur = lambda b, c: b * nc + jnp.minimum(c, nc - 1)
    prv = lambda b, c: b * nc + jnp.maximum(c - 1, 0)
    blk8 = lambda base, chunk: pl.BlockSpec((RW_BLOCKS, c_len, LANES),
                                            lambda b, c: (base // RW_BLOCKS, chunk(b, c), 0))
    vec8 = pl.BlockSpec((RW_BLOCKS, 1, LANES), lambda b, c: (0, 0, 0))
    lora = pl.BlockSpec((LANES, RW_WIDTH), lambda b, c: (0, 0))
    state_spec = pl.BlockSpec((1, RW_HEADS, RW_HEAD, LANES), lambda b, c: (b, 0, 0, 0))
    shift_spec = pl.BlockSpec((1, SHIFT_BLOCKS, 1, LANES), lambda b, c: (b, 0, 0, 0))
    return pl.pallas_call(
        kern,
        out_shape=(
            jax.ShapeDtypeStruct((n, RW_WIDTH), BF16),
            jax.ShapeDtypeStruct((batch, RW_HEADS, RW_HEAD, LANES), F32),
            jax.ShapeDtypeStruct((batch, SHIFT_BLOCKS, 1, LANES), F32),
        ),
        grid=(batch, nc + 1),
        in_specs=[
            blk8(RB0, cur), blk8(KRB0, cur), blk8(VRB0, cur), blk8(GRB0, prv),
            pl.BlockSpec((1, c_len, LANES), lambda b, c: (WAB0, cur(b, c), 0)),
            shift_spec, state_spec,
            pl.BlockSpec((SHIFT_BLOCKS, 1, LANES), lambda b, c: (0, 0, 0)),
            vec8, vec8, vec8, vec8, vec8, vec8, vec8, lora, lora,
        ],
        out_specs=(pl.BlockSpec((c_len, RW_WIDTH), lambda b, c: (prv(b, c), 0)), state_spec, shift_spec),
        scratch_shapes=[
            pltpu.VMEM((RW_HEADS, RW_HEAD, LANES), F32),
            pltpu.VMEM((SHIFT_BLOCKS, 1, LANES), F32),
            pltpu.VMEM((RW_HEADS, 2 * c_len, LANES), BF16),
            pltpu.VMEM((RW_BLOCKS, 2 * c_len, LANES), BF16),
            pltpu.VMEM((RW_HEADS, c_len, LANES), BF16),
            pltpu.VMEM((RW_BLOCKS, LANES, 2 * c_len), BF16),
            pltpu.VMEM((RW_BLOCKS, 1, LANES), F32),
            pltpu.VMEM((RW_BLOCKS, c_len, LANES), F32),
        ],
        compiler_params=_cparams(("parallel", "arbitrary")),
        name="rwkv7_chunked",
    )(proj, proj, proj, proj, proj, shift0_b, wkv0_p, prm["mu"], prm["w0"], prm["a0"], prm["k_k"], prm["k_a"],
      prm["r_k"], prm["lnx_g"], prm["lnx_b"], prm["w_up"], prm["a_up"])


def _pack_state(wkv):
    h = jnp.swapaxes(wkv, -1, -2)
    z = jnp.zeros_like(h)
    even = jnp.concatenate([z, h], axis=-1)
    odd = jnp.concatenate([h, z], axis=-1)
    is_even = (jnp.arange(RW_HEADS) % 2 == 0)[None, :, None, None]
    return jnp.where(is_even, even, odd)


def _unpack_state(hp):
    is_even = (jnp.arange(RW_HEADS) % 2 == 0)[None, :, None, None]
    h = jnp.where(is_even, hp[..., RW_HEAD:], hp[..., :RW_HEAD])
    return jnp.swapaxes(h, -1, -2)


def _out_proj_kernel(att_ref, rw_ref, wa_ref, wr_ref, x_ref, o_ref):
    acc = jnp.dot(att_ref[...], wa_ref[...], preferred_element_type=F32)
    acc = acc + jnp.dot(rw_ref[...], wr_ref[...], preferred_element_type=F32)
    o_ref[...] = x_ref[...] + acc


def _out_proj(att, rw, w_att, w_rw, x2, *, tm):
    n = x2.shape[0]
    return pl.pallas_call(
        _out_proj_kernel,
        out_shape=jax.ShapeDtypeStruct((n, D_MODEL), F32),
        grid=(n // tm,),
        in_specs=[
            pl.BlockSpec((tm, DA_WIDTH), lambda i: (i, 0)),
            pl.BlockSpec((tm, RW_WIDTH), lambda i: (i, 0)),
            pl.BlockSpec((DA_WIDTH, D_MODEL), lambda i: (0, 0)),
            pl.BlockSpec((RW_WIDTH, D_MODEL), lambda i: (0, 0)),
            pl.BlockSpec((tm, D_MODEL), lambda i: (i, 0)),
        ],
        out_specs=pl.BlockSpec((tm, D_MODEL), lambda i: (i, 0)),
        compiler_params=_cparams(("parallel",)),
        name="out_proj",
    )(att, rw, w_att, w_rw, x2)


def _blocks(v):
    return v.reshape(-1, 1, LANES)


def _prepare_params(norm_g, w_in, q_norm_g, k_norm_g, lq1, lk1, lq2, lk2, subln_g, shift_mu, w0, w_up, a0, a_up,
                    k_k, k_a, r_k, lnx_g, lnx_b, w_out):
    rw0 = 4 * DA_WIDTH
    lora0 = rw0 + 3 * RW_WIDTH
    g0 = rw0 + RW_SHIFT_COLS
    w_perm = jnp.concatenate([w_in[:, :lora0], w_in[:, g0:], w_in[:, lora0:g0],
                              jnp.zeros((D_MODEL, N_COL_BLOCKS * LANES - IN_COLS), w_in.dtype)], axis=1)
    half = DA_QK_DIM // 2
    inv_freq = ROPE_THETA ** (-jnp.arange(half, dtype=F32) / half)
    zeros = jnp.zeros((RW_LORA, RW_WIDTH), F32)
    return dict(
        norm_g=norm_g.reshape(1, D_MODEL),
        w_in=w_perm.astype(BF16),
        gq=jnp.tile(q_norm_g, 2).reshape(1, LANES),
        gk=jnp.tile(k_norm_g, 2).reshape(1, LANES),
        invf=jnp.tile(inv_freq, 4).reshape(1, LANES),
        lam4=jnp.stack([lq1, lk1, lq2, lk2]),
        score_bound=1.02 * LOG2_E * DA_QK_DIM ** 0.5 * jnp.max(jnp.abs(q_norm_g)) * jnp.max(jnp.abs(k_norm_g)),
        subln_col=subln_g.reshape(LANES, 1),
        subln_row=subln_g.reshape(1, LANES),
        mu=_blocks(shift_mu),
        w0=_blocks(w0), a0=_blocks(a0), k_k=_blocks(k_k), k_a=_blocks(k_a), r_k=_blocks(r_k.reshape(-1)),
        lnx_g=_blocks(lnx_g), lnx_b=_blocks(lnx_b),
        w_up=jnp.concatenate([w_up, zeros], axis=0).astype(BF16),
        a_up=jnp.concatenate([zeros, a_up], axis=0).astype(BF16),
        w_att=w_out[:DA_WIDTH].astype(BF16),
        w_rw=w_out[DA_WIDTH:].astype(BF16),
    )


def _tiles(batch, t_len, past_len):
    n = batch * t_len
    if past_len == 0:
        tq = min(t_len, 512)
        tk = tv = tq // 2
    else:
        tq = tk = tv = t_len
    return dict(tm=min(n, 1024), bn=6, tq=tq, tk=tk, tv=tv, c_len=min(t_len, CHUNK), tm_out=min(n, 512),
                k_time_minor=tq % LANES == 0)


def _layer(x, past_k, past_v, wkv0, shift0, layer_idx, prm):
    batch, t_len, _ = x.shape
    past_len = past_k.shape[1]
    n = batch * t_len
    assert t_len & (t_len - 1) == 0
    tl = _tiles(batch, t_len, past_len)
    tq, tk = tl["tq"], tl["tk"]
    lam_init = 0.8 - 0.6 * math.exp(-0.3 * layer_idx)

    x2 = x.reshape(n, D_MODEL)
    proj = _in_proj(x2, prm["norm_g"], prm["w_in"], tm=tl["tm"], bn=tl["bn"])
    k_out, v_out, qs_bf, k_bf, vt_bf = _qkv_prep(proj, prm["gq"], prm["gk"], prm["invf"], batch=batch, t_len=t_len,
                                                 pos0=past_len, tq=tq, tv=tl["tv"],
                                                 k_time_minor=tl["k_time_minor"])
    if tl["k_time_minor"]:
        k_out = k_out.transpose(0, 4, 1, 2, 3)

    if past_len:
        cache_kt = past_k.transpose(0, 2, 3, 4, 1).reshape(batch, DA_HEADS, LANES, past_len)
        cache_v = past_v.reshape(batch, past_len * DA_HEADS, LANES)
        att = _attention_decode(prm["lam4"], qs_bf, cache_kt, cache_v, k_bf, v_out, proj, prm["subln_row"],
                                batch=batch, t_q=t_len, lam_init=lam_init)
    else:
        att = _attention(prm["lam4"], qs_bf, k_bf.reshape(DA_HEADS, batch, t_len, LANES), vt_bf, proj,
                         prm["subln_col"], prm["score_bound"], batch=batch, t_q=t_len, t_k=t_len, tq=tq, tk=tk,
                         q_pos0=0, kv_len=t_len, lam_init=lam_init)

    shift0_b = shift0.reshape(batch, SHIFT_BLOCKS, 1, LANES)
    rw, state_p, shift_b = _rwkv(proj, shift0_b, _pack_state(wkv0), prm, batch=batch, t_len=t_len,
                                 c_len=tl["c_len"])

    y = _out_proj(att, rw, prm["w_att"], prm["w_rw"], x2, tm=tl["tm_out"])
    return (y.reshape(batch, t_len, D_MODEL),
            k_out.reshape(batch, t_len, DA_HEADS, 2, DA_QK_DIM),
            v_out.reshape(batch, t_len, DA_HEADS, DA_V_DIM),
            _unpack_state(state_p),
            shift_b.reshape(batch, 1, RW_SHIFT_COLS))


def kernel(x_prompt, x_sample, cache_attn_k, cache_attn_v, state_rwkv_wkv, state_rwkv_shift, norm_g, w_in, q_norm_g, k_norm_g, lambda_q1, lambda_k1, lambda_q2, lambda_k2, subln_g, shift_mu, w0, w_up, a0, a_up, k_k, k_a, r_k, lnx_g, lnx_b, w_out):
    depth = w_in.shape[0]
    batch = x_prompt.shape[0]
    dt = x_prompt.dtype
    yp, ys = x_prompt, x_sample
    outs = [[] for _ in range(8)]
    for l in range(depth):
        prm = _prepare_params(norm_g[l], w_in[l], q_norm_g[l], k_norm_g[l], lambda_q1[l], lambda_k1[l], lambda_q2[l],
                              lambda_k2[l], subln_g[l], shift_mu[l], w0[l], w_up[l], a0[l], a_up[l], k_k[l], k_a[l],
                              r_k[l], lnx_g[l], lnx_b[l], w_out[l])
        yp, kp, vp, sp, hp = _layer(
            yp, jnp.zeros((batch, 0, DA_HEADS, 2, DA_QK_DIM), dt), jnp.zeros((batch, 0, DA_HEADS, DA_V_DIM), dt),
            jnp.zeros((batch, RW_HEADS, RW_HEAD, RW_HEAD), dt), jnp.zeros((batch, 1, RW_SHIFT_COLS), dt), l, prm)
        ys, ksn, vsn, ssn, hsn = _layer(ys, cache_attn_k[l], cache_attn_v[l], state_rwkv_wkv[l],
                                        state_rwkv_shift[l], l, prm)
        for lst, val in zip(outs, (kp, vp, sp, hp, ksn, vsn, ssn, hsn)):
            lst.append(val)
    return (yp, ys) + tuple(jnp.stack(o) for o in outs)
```
